```python
import math
import jax, jax.numpy as jnp
from jax import lax
import numpy as np

D_MODEL = 2048
BATCH = 2
SEQ = 4096
DEPTH = 2
DEC_BATCH = 1
DEC_SEQ = 8192
PAST_LEN = 128

GRID_W = 64
Q_BLOCK = 128
EPS = 1e-6
ROPE_THETA = 10000.0
NUM_BUCKETS = 32
MAX_DISTANCE = 128
A_HEADS = 8
A_QK_DIM = 64
A_V_DIM = 128
A_OUT = A_HEADS * A_V_DIM
B_Q_HEADS = 8
B_KV_HEADS = 2
B_GROUP = B_Q_HEADS // B_KV_HEADS
B_DIM = 128
B_OUT = B_Q_HEADS * B_DIM
C_HEADS = 8
C_Q_RANK = 512
C_KV_RANK = 256
C_NOPE = 128
C_ROPE = 64
C_V = 128
C_OUT = C_HEADS * C_V
N_BRANCH = 3
X_HEADS = 4
X_DIM = 128
MEM_TOKENS = 256
D_FF = -(-8 * D_MODEL // (3 * 256)) * 256
IN_SIZES = (A_HEADS * 2 * A_QK_DIM, A_HEADS * 2 * A_QK_DIM, A_OUT,
            B_Q_HEADS * B_DIM, B_KV_HEADS * B_DIM, B_KV_HEADS * B_DIM,
            C_Q_RANK, C_KV_RANK, C_ROPE, N_BRANCH * D_MODEL)
IN_COLS = sum(IN_SIZES)

kernel_name = "hybrid_bidir_encoder_gated_parallel"

F32 = jnp.float32


def rms_norm(x, g):
    xf = x.astype(F32)
    y = xf * lax.rsqrt(jnp.mean(xf * xf, axis=-1, keepdims=True) + EPS)
    return (y * g.astype(F32)).astype(x.dtype)


def rope(x, pos):
    d = x.shape[-1]
    inv = ROPE_THETA ** (-jnp.arange(0, d, 2, dtype=F32) / d)
    ang = pos.astype(F32)[:, None] * inv[None, :]
    cos, sin = jnp.cos(ang).astype(x.dtype), jnp.sin(ang).astype(x.dtype)
    x1, x2 = jnp.split(x, 2, axis=-1)
    return jnp.concatenate([x1 * cos - x2 * sin, x1 * sin + x2 * cos], axis=-1)


def axial_rope(x, row_pos, col_pos):
    half = x.shape[-1] // 2
    return jnp.concatenate([rope(x[..., :half], row_pos), rope(x[..., half:], col_pos)], axis=-1)


def t5_bucket(rel):
    nb = NUM_BUCKETS // 2
    max_exact = nb // 2
    ret = (rel > 0).astype(jnp.int32) * nb
    n = jnp.abs(rel)
    nf = jnp.maximum(n, 1).astype(F32)
    large = max_exact + (jnp.log(nf / max_exact) / math.log(MAX_DISTANCE / max_exact)
                         * (nb - max_exact)).astype(jnp.int32)
    large = jnp.minimum(large, nb - 1)
    return ret + jnp.where(n < max_exact, n, large)


def sweep_blocks(block_fn, seq_len):
    out = lax.map(block_fn, jnp.arange(seq_len // Q_BLOCK))
    nblk, b, h, qb, dv = out.shape
    return out.transpose(1, 0, 3, 2, 4).reshape(b, nblk * qb, h, dv)


def diff_attention(q1, q2, k1, k2, v, rel_bias, lam):
    S = q1.shape[2]
    scale = A_QK_DIM ** -0.5
    kpos = jnp.arange(S)
    table = rel_bias.T.astype(F32)

    def block(i):
        s0 = i * Q_BLOCK
        qa = lax.dynamic_slice_in_dim(q1, s0, Q_BLOCK, axis=2)
        qb = lax.dynamic_slice_in_dim(q2, s0, Q_BLOCK, axis=2)
        qpos = s0 + jnp.arange(Q_BLOCK)
        bias = table[:, t5_bucket(kpos[None, :] - qpos[:, None])]
        p1 = jax.nn.softmax(jnp.einsum('bhqd,bhkd->bhqk', qa, k1, preferred_element_type=F32) * scale + bias, axis=-1)
        p2 = jax.nn.softmax(jnp.einsum('bhqd,bhkd->bhqk', qb, k2, preferred_element_type=F32) * scale + bias, axis=-1)
        return jnp.einsum('bhqk,bhkd->bhqd', (p1 - lam * p2).astype(v.dtype), v)

    return sweep_blocks(block, S)


def gqa_attention(q, k, v):
    b, n, g, S, d = q.shape
    scale = d ** -0.5

    def block(i):
        qb = lax.dynamic_slice_in_dim(q, i * Q_BLOCK, Q_BLOCK, axis=3)
        s = jnp.einsum('bngqd,bnkd->bngqk', qb, k, preferred_element_type=F32) * scale
        p = jax.nn.softmax(s, axis=-1)
        o = jnp.einsum('bngqk,bnkd->bngqd', p.astype(v.dtype), v)
        return o.reshape(b, n * g, Q_BLOCK, d)

    return sweep_blocks(block, S)


def mla_attention(q_nope, q_rope, k_nope, k_rope, v):
    S = q_nope.shape[2]
    scale = (C_NOPE + C_ROPE) ** -0.5

    def block(i):
        s0 = i * Q_BLOCK
        qn = lax.dynamic_slice_in_dim(q_nope, s0, Q_BLOCK, axis=2)
        qr = lax.dynamic_slice_in_dim(q_rope, s0, Q_BLOCK, axis=2)
        s = (jnp.einsum('bhqd,bhkd->bhqk', qn, k_nope, preferred_element_type=F32)
             + jnp.einsum('bhqr,bkr->bhqk', qr, k_rope, preferred_element_type=F32)) * scale
        p = jax.nn.softmax(s, axis=-1)
        return jnp.einsum('bhqk,bhkd->bhqd', p.astype(v.dtype), v)

    return sweep_blocks(block, S)


def memory_attention(h, m, w_q, w_kv, w_out):
    b, S, _ = h.shape
    M = m.shape[1]
    q = (h @ w_q).reshape(b, S, X_HEADS, X_DIM)
    kv = (m @ w_kv).reshape(b, M, 2, X_HEADS, X_DIM)
    k, v = kv[:, :, 0], kv[:, :, 1]
    s = jnp.einsum('bqhd,bkhd->bhqk', q, k, preferred_element_type=F32) * X_DIM ** -0.5
    p = jax.nn.softmax(s, axis=-1)
    o = jnp.einsum('bhqk,bkhd->bqhd', p.astype(v.dtype), v).reshape(b, S, X_HEADS * X_DIM)
    return o @ w_out


def _trunk(x, mem, rel_bias, g_mix_pre, g_mix_post, w_in, lam_q1, lam_k1, lam_q2, lam_k2, g_a_out,
           g_b_q, g_b_k, g_c_q, g_c_kv, w_c_q_up, w_c_kv_up, w_br_a, w_br_b, w_br_c, w_mix_out,
           g_x_pre, g_x_post, g_mem, w_x_q, w_x_kv, w_x_out,
           g_ffn_pre, g_ffn_post, w_ffn_gate, w_ffn_up, w_ffn_down):
    b, S, _ = x.shape
    rows = S // GRID_W
    rr, cc = jnp.meshgrid(jnp.arange(rows), jnp.arange(GRID_W), indexing='ij')
    row_pos, col_pos = rr.reshape(-1), cc.reshape(-1)
    tok_pos = jnp.arange(S)
    split_points = [int(v) for v in np.cumsum(IN_SIZES)[:-1]]
    for l in range(DEPTH):
        h = rms_norm(x, g_mix_pre[l])
        z = h @ w_in[l]
        a_q, a_k, a_v, b_q, b_k, b_v, c_qa, c_kva, c_kr, gates = jnp.split(z, split_points, axis=-1)

        lam_init = 0.8 - 0.6 * math.exp(-0.3 * l)
        lam = (jnp.exp(jnp.sum(lam_q1[l].astype(F32) * lam_k1[l].astype(F32)))
               - jnp.exp(jnp.sum(lam_q2[l].astype(F32) * lam_k2[l].astype(F32))) + lam_init)
        qa = a_q.reshape(b, S, A_HEADS, 2, A_QK_DIM).transpose(3, 0, 2, 1, 4)
        ka = a_k.reshape(b, S, A_HEADS, 2, A_QK_DIM).transpose(3, 0, 2, 1, 4)
        va = a_v.reshape(b, S, A_HEADS, A_V_DIM).transpose(0, 2, 1, 3)
        oa = diff_attention(qa[0], qa[1], ka[0], ka[1], va, rel_bias, lam)
        oa = (rms_norm(oa, g_a_out[l]) * (1.0 - lam_init)).reshape(b, S, A_OUT)

        qb = rms_norm(b_q.reshape(b, S, B_Q_HEADS, B_DIM), g_b_q[l]).transpose(0, 2, 1, 3)
        kb = rms_norm(b_k.reshape(b, S, B_KV_HEADS, B_DIM), g_b_k[l]).transpose(0, 2, 1, 3)
        vb = b_v.reshape(b, S, B_KV_HEADS, B_DIM).transpose(0, 2, 1, 3)
        qb = axial_rope(qb, row_pos, col_pos)
        kb = axial_rope(kb, row_pos, col_pos)
        ob = gqa_attention(qb.reshape(b, B_KV_HEADS, B_GROUP, S, B_DIM), kb, vb).reshape(b, S, B_OUT)

        cq = (rms_norm(c_qa, g_c_q[l]) @ w_c_q_up[l]).reshape(b, S, C_HEADS, C_NOPE + C_ROPE).transpose(0, 2, 1, 3)
        q_nope, q_rope = cq[..., :C_NOPE], rope(cq[..., C_NOPE:], tok_pos)
        ckv = (rms_norm(c_kva, g_c_kv[l]) @ w_c_kv_up[l]).reshape(b, S, C_HEADS, C_NOPE + C_V).transpose(0, 2, 1, 3)
        k_nope, vc = ckv[..., :C_NOPE], ckv[..., C_NOPE:]
        k_rope = rope(c_kr, tok_pos)
        oc = mla_attention(q_nope, q_rope, k_nope, k_rope, vc).reshape(b, S, C_OUT)

        g = jax.nn.sigmoid(gates.astype(F32)).astype(x.dtype).reshape(b, S, N_BRANCH, D_MODEL)
        merged = (g[:, :, 0] * (oa @ w_br_a[l]) + g[:, :, 1] * (ob @ w_br_b[l])
                  + g[:, :, 2] * (oc @ w_br_c[l]))
        x = x + rms_norm(merged @ w_mix_out[l], g_mix_post[l])

        h = rms_norm(x, g_x_pre[l])
        xo = memory_attention(h, rms_norm(mem, g_mem[l]), w_x_q[l], w_x_kv[l], w_x_out[l])
        x = x + rms_norm(xo, g_x_post[l])

        h = rms_norm(x, g_ffn_pre[l])
        f = (jax.nn.silu(h @ w_ffn_gate[l]) * (h @ w_ffn_up[l])) @ w_ffn_down[l]
        x = x + rms_norm(f, g_ffn_post[l])
    return x


def setup_inputs(seed: int = 0) -> dict:
    key = jax.random.key(seed)
    ks = iter(jax.random.split(key, 40))

    def nrm(shape, scale):
        return jax.random.normal(next(ks), shape, F32) * scale

    def gain(shape):
        return 1.0 + 0.02 * jax.random.normal(next(ks), shape, F32)

    L, D = DEPTH, D_MODEL
    return {
        "x_prompt": nrm((BATCH, SEQ, D), 1.0),
        "x_sample": nrm((DEC_BATCH, DEC_SEQ, D), 1.0),
        "mem_prompt": nrm((BATCH, MEM_TOKENS, D), 1.0),
        "mem_sample": nrm((DEC_BATCH, MEM_TOKENS, D), 1.0),
        "rel_bias": nrm((NUM_BUCKETS, A_HEADS), 0.5),
        "g_mix_pre": gain((L, D)),
        "g_mix_post": gain((L, D)),
        "w_in": nrm((L, D, IN_COLS), D ** -0.5),
        "lam_q1": nrm((L, A_QK_DIM), 0.1),
        "lam_k1": nrm((L, A_QK_DIM), 0.1),
        "lam_q2": nrm((L, A_QK_DIM), 0.1),
        "lam_k2": nrm((L, A_QK_DIM), 0.1),
        "g_a_out": gain((L, A_V_DIM)),
        "g_b_q": gain((L, B_DIM)),
        "g_b_k": gain((L, B_DIM)),
        "g_c_q": gain((L, C_Q_RANK)),
        "g_c_kv": gain((L, C_KV_RANK)),
        "w_c_q_up": nrm((L, C_Q_RANK, C_HEADS * (C_NOPE + C_ROPE)), C_Q_RANK ** -0.5),
        "w_c_kv_up": nrm((L, C_KV_RANK, C_HEADS * (C_NOPE + C_V)), C_KV_RANK ** -0.5),
        "w_br_a": nrm((L, A_OUT, D), A_OUT ** -0.5),
        "w_br_b": nrm((L, B_OUT, D), B_OUT ** -0.5),
        "w_br_c": nrm((L, C_OUT, D), C_OUT ** -0.5),
        "w_mix_out": nrm((L, D, D), D ** -0.5),
        "g_x_pre": gain((L, D)),
        "g_x_post": gain((L, D)),
        "g_mem": gain((L, D)),
        "w_x_q": nrm((L, D, X_HEADS * X_DIM), D ** -0.5),
        "w_x_kv": nrm((L, D, 2 * X_HEADS * X_DIM), D ** -0.5),
        "w_x_out": nrm((L, X_HEADS * X_DIM, D), (X_HEADS * X_DIM) ** -0.5),
        "g_ffn_pre": gain((L, D)),
        "g_ffn_post": gain((L, D)),
        "w_ffn_gate": nrm((L, D, D_FF), D ** -0.5),
        "w_ffn_up": nrm((L, D, D_FF), D ** -0.5),
        "w_ffn_down": nrm((L, D_FF, D), D_FF ** -0.5),
    }


def reference(x_prompt, x_sample, mem_prompt, mem_sample, rel_bias, g_mix_pre, g_mix_post, w_in,
              lam_q1, lam_k1, lam_q2, lam_k2, g_a_out, g_b_q, g_b_k, g_c_q, g_c_kv, w_c_q_up, w_c_kv_up,
              w_br_a, w_br_b, w_br_c, w_mix_out, g_x_pre, g_x_post, g_mem, w_x_q, w_x_kv, w_x_out,
              g_ffn_pre, g_ffn_post, w_ffn_gate, w_ffn_up, w_ffn_down):
    weights = (rel_bias, g_mix_pre, g_mix_post, w_in, lam_q1, lam_k1, lam_q2, lam_k2, g_a_out,
               g_b_q, g_b_k, g_c_q, g_c_kv, w_c_q_up, w_c_kv_up, w_br_a, w_br_b, w_br_c, w_mix_out,
               g_x_pre, g_x_post, g_mem, w_x_q, w_x_kv, w_x_out,
               g_ffn_pre, g_ffn_post, w_ffn_gate, w_ffn_up, w_ffn_down)
    y_prompt = _trunk(x_prompt, mem_prompt, *weights)
    y_sample = _trunk(x_sample, mem_sample, *weights)
    return (y_prompt, y_sample)
```

```python
import functools
import math

import jax
import jax.numpy as jnp
from jax import lax
from jax.experimental import pallas as pl
from jax.experimental.pallas import tpu as pltpu

F32 = jnp.float32
BF16 = jnp.bfloat16

EPS = 1e-6
ROPE_THETA = 10000.0
LOG2E = 1.4426950408889634
GRID_W = 64
NUM_BUCKETS = 32
MAX_DISTANCE = 128

D_MODEL = 2048
HEADS = 8
HEAD_DIM = 128
A_QK_DIM = 64
B_KV_HEADS = 2
C_Q_RANK = 512
C_KV_RANK = 256
C_NOPE = 128
C_ROPE = 64
X_HEADS = 4
MEM_TOKENS = 256

Z_AQ, Z_AK, Z_AV, Z_BQ, Z_BK, Z_BV, Z_CQA, Z_CKVA, Z_CKR = 0, 1024, 2048, 3072, 4096, 4352, 4608, 5120, 5376
Z_RAW_SPLIT = 5440
Z_GATES = 5632
Z_COLS = Z_GATES + 3 * D_MODEL

VMEM_LIMIT = 56 * 1024 * 1024


def _cparams(sem):
    return pltpu.CompilerParams(dimension_semantics=sem, vmem_limit_bytes=VMEM_LIMIT)


def _rms(x, g):
    return x * lax.rsqrt(jnp.mean(x * x, axis=-1, keepdims=True) + EPS) * g


def _swap32(x):
    lane = lax.broadcasted_iota(jnp.int32, x.shape, 1)
    return jnp.where((lane % 64) < 32, pltpu.roll(x, 96, 1), pltpu.roll(x, 32, 1))


def _rope(x, cos, sin_signed):
    return x * cos + _swap32(x) * sin_signed


def _rms_matmul_kernel(x_ref, g_ref, w_ref, cs_ref, o_ref, xn_ref):
    @pl.when(pl.program_id(1) == 0)
    def _():
        xn_ref[...] = _rms(x_ref[...].astype(F32), g_ref[...]).astype(BF16)

    acc = jnp.dot(xn_ref[...], w_ref[...], preferred_element_type=F32)
    o_ref[...] = (acc * cs_ref[...]).astype(o_ref.dtype)


def rms_matmul(x, g, w, colscale, tm, tn, out_dtype=BF16):
    m, k = x.shape
    n = w.shape[1]
    return pl.pallas_call(
        _rms_matmul_kernel,
        grid=(m // tm, n // tn),
        in_specs=[
            pl.BlockSpec((tm, k), lambda i, j: (i, 0)),
            pl.BlockSpec((1, k), lambda i, j: (0, 0)),
            pl.BlockSpec((k, tn), lambda i, j: (0, j)),
            pl.BlockSpec((1, tn), lambda i, j: (0, j)),
        ],
        out_specs=pl.BlockSpec((tm, tn), lambda i, j: (i, j)),
        out_shape=jax.ShapeDtypeStruct((m, n), out_dtype),
        scratch_shapes=[pltpu.VMEM((tm, k), BF16)],
        compiler_params=_cparams(("parallel", "arbitrary")),
        name="rms_matmul",
    )(x, g, w, colscale)


def _prep_kernel(bq_ref, bkv_ref, cqa_ref, ckk_ref, cos_a_ref, sin_a_ref, cos_r_ref, sin_r_ref,
                 gbq_ref, gbk_ref, gcq_ref, gckv_ref, wq_ref, wkv_ref,
                 qb_out, kb_out, cq_out, ckv_out, kr_out, *, b_scale, c_scale):
    cos_a, sin_a = cos_a_ref[...], sin_a_ref[...]
    cos_r, sin_r = cos_r_ref[...], sin_r_ref[...]

    for h in range(HEADS):
        sl = slice(h * HEAD_DIM, (h + 1) * HEAD_DIM)
        y = _rms(bq_ref[:, sl].astype(F32), gbq_ref[...])
        qb_out[:, sl] = (_rope(y, cos_a, sin_a) * b_scale).astype(BF16)
    for n in range(B_KV_HEADS):
        sl = slice(n * HEAD_DIM, (n + 1) * HEAD_DIM)
        y = _rms(bkv_ref[:, sl].astype(F32), gbk_ref[...])
        kb_out[:, sl] = _rope(y, cos_a, sin_a).astype(BF16)

    xq = _rms(cqa_ref[...].astype(F32), gcq_ref[...]).astype(BF16)
    cq = jnp.dot(xq, wq_ref[...], preferred_element_type=F32)
    nope_cols = HEADS * C_NOPE
    cq_out[:, :nope_cols] = (cq[:, :nope_cols] * c_scale).astype(BF16)
    for h in range(HEADS):
        sl = slice(nope_cols + h * HEAD_DIM, nope_cols + (h + 1) * HEAD_DIM)
        cq_out[:, sl] = (_rope(cq[:, sl], cos_r, sin_r) * c_scale).astype(BF16)

    xkv = _rms(ckk_ref[:, :C_KV_RANK].astype(F32), gckv_ref[...]).astype(BF16)
    ckv_out[...] = jnp.dot(xkv, wkv_ref[...], preferred_element_type=F32).astype(BF16)
    kr = ckk_ref[:, C_KV_RANK:C_KV_RANK + HEAD_DIM].astype(F32)
    kr_out[...] = _rope(kr, cos_r, sin_r).astype(BF16)


def mixer_prep(z, tables, pos_block, gbq, gbk, gcq, gckv, wq, wkv, tm):
    t = z.shape[0]
    cos_a, sin_a, cos_r, sin_r = tables
    row = lambda w: pl.BlockSpec((tm, w), lambda i: (i, 0))
    tab = pl.BlockSpec((tm, HEAD_DIM), lambda i: (pos_block(i), 0))
    full = lambda a: pl.BlockSpec(a.shape, lambda i: (0,) * a.ndim)
    kern = functools.partial(_prep_kernel, b_scale=HEAD_DIM ** -0.5 * LOG2E,
                             c_scale=(C_NOPE + C_ROPE) ** -0.5 * LOG2E)
    return pl.pallas_call(
        kern,
        grid=(t // tm,),
        in_specs=[
            pl.BlockSpec((tm, 1024), lambda i: (i, Z_BQ // 1024)),
            pl.BlockSpec((tm, 512), lambda i: (i, Z_BK // 512)),
            pl.BlockSpec((tm, 512), lambda i: (i, Z_CQA // 512)),
            pl.BlockSpec((tm, 512), lambda i: (i, Z_CKVA // 512)),
            tab, tab, tab, tab,
            full(gbq), full(gbk), full(gcq), full(gckv), full(wq), full(wkv),
        ],
        out_specs=[row(1024), row(256), row(2048), row(2048), row(HEAD_DIM)],
        out_shape=[
            jax.ShapeDtypeStruct((t, 1024), BF16),
            jax.ShapeDtypeStruct((t, 256), BF16),
            jax.ShapeDtypeStruct((t, 2048), BF16),
            jax.ShapeDtypeStruct((t, 2048), BF16),
            jax.ShapeDtypeStruct((t, HEAD_DIM), BF16),
        ],
        compiler_params=_cparams(("parallel",)),
        name="mixer_prep",
    )(z, z, z, z, cos_a, sin_a, cos_r, sin_r, gbq, gbk, gcq, gckv, wq, wkv)


def _flash(q, k_refs, v_ref, nk, tk, bias_fn=None):
    m_rows = q.shape[0]

    def body(i, carry):
        m, l, acc = carry
        off = pl.multiple_of(i * tk, tk)
        ks = [r[pl.ds(off, tk), :] for r in k_refs]
        k = ks[0] if len(ks) == 1 else jnp.concatenate(ks, axis=1)
        s = lax.dot_general(q, k, (((1,), (1,)), ((), ())), preferred_element_type=F32)
        if bias_fn is not None:
            s = bias_fn(i, s)
        m_new = jnp.maximum(m, jnp.max(s, axis=-1, keepdims=True))
        p = jnp.exp2(s - m_new)
        alpha = jnp.exp2(m - m_new)
        l = alpha * l + jnp.sum(p, axis=-1, keepdims=True)
        pv = jnp.dot(p.astype(BF16), v_ref[pl.ds(off, tk), :], preferred_element_type=F32)
        return m_new, l, alpha * acc + pv

    init = (jnp.full((m_rows, 1), -jnp.inf, F32), jnp.zeros((m_rows, 1), F32),
            jnp.zeros((m_rows, v_ref.shape[1]), F32))
    return lax.fori_loop(0, nk, body, init)


def _attn_kernel(*refs, n_q, n_k, nk, tk):
    q_refs, k_refs = refs[:n_q], refs[n_q:n_q + n_k]
    v_ref, o_ref = refs[n_q + n_k], refs[-1]
    qs = [r[...] for r in q_refs]
    q = qs[0] if n_q == 1 else jnp.concatenate(qs, axis=1)
    _, l, acc = _flash(q, k_refs, v_ref, nk, tk)
    o_ref[...] = (acc / l).astype(o_ref.dtype)


def _attn_a_kernel(lam_ref, q_ref, k_ref, v_ref, bias_ref, g_ref, *rest, tq, tk, nk, r, post_scale):
    o_ref = rest[-1]
    qi = pl.program_id(2)
    q = q_ref[...]
    lane = lax.broadcasted_iota(jnp.int32, q.shape, 1)
    zero = jnp.zeros_like(q)
    q2 = jnp.concatenate([jnp.where(lane < A_QK_DIM, q, zero), jnp.where(lane >= A_QK_DIM, q, zero)], axis=0)

    def bias_fn(i, s):
        idx = jnp.clip(i * r - qi, -r - 1, 2) + r + 1
        b = bias_ref[0, idx]
        return jnp.concatenate([s[:tq] + b, s[tq:] + b], axis=0)

    _, l, acc = _flash(q2, [k_ref], v_ref, nk, tk, bias_fn)
    o = acc / l
    d = o[:tq] - lam_ref[0] * o[tq:]
    o_ref[...] = (_rms(d, g_ref[...]) * post_scale).astype(o_ref.dtype)


def _alias_prev(in_specs, args, prev):
    if prev is None:
        return in_specs, args, {}
    return in_specs + [pl.BlockSpec(memory_space=pl.ANY)], args + [prev], {len(args): 0}


def attention(q_parts, k_parts, v_part, t, seq, row_off, nseg, tq, tk, prev):
    nq = seq // tq
    qoff, koff = row_off // tq, row_off // seq
    in_specs, args = [], []
    for a, cf in q_parts:
        in_specs.append(pl.BlockSpec((tq, HEAD_DIM), lambda s, h, i, cf=cf: (qoff + s * nq + i, cf(h))))
        args.append(a)
    for a, cf in list(k_parts) + [v_part]:
        in_specs.append(pl.BlockSpec((seq, HEAD_DIM), lambda s, h, i, cf=cf: (koff + s, cf(h))))
        args.append(a)
    in_specs, args, aliases = _alias_prev(in_specs, args, prev)
    kern = functools.partial(_attn_kernel, n_q=len(q_parts), n_k=len(k_parts), nk=seq // tk, tk=tk)
    return pl.pallas_call(
        kern,
        grid=(nseg, HEADS, nq),
        in_specs=in_specs,
        out_specs=pl.BlockSpec((tq, HEAD_DIM), lambda s, h, i: (qoff + s * nq + i, h)),
        out_shape=jax.ShapeDtypeStruct((t, HEADS * HEAD_DIM), BF16),
        input_output_aliases=aliases,
        compiler_params=_cparams(("parallel", "parallel", "arbitrary")),
        name="attention",
    )(*args)


def attention_a(lam, z, bias, g, seq, row_off, nseg, tq, tk, post_scale, prev):
    t = z.shape[0]
    nq = seq // tq
    r = tk // tq
    qoff, koff = row_off // tq, row_off // seq
    in_specs = [
        pl.BlockSpec(memory_space=pltpu.SMEM),
        pl.BlockSpec((tq, HEAD_DIM), lambda s, h, i: (qoff + s * nq + i, Z_AQ // HEAD_DIM + h)),
        pl.BlockSpec((seq, HEAD_DIM), lambda s, h, i: (koff + s, Z_AK // HEAD_DIM + h)),
        pl.BlockSpec((seq, HEAD_DIM), lambda s, h, i: (koff + s, Z_AV // HEAD_DIM + h)),
        pl.BlockSpec((1, r + 4, tq, tk), lambda s, h, i: (h, 0, 0, 0)),
        pl.BlockSpec((1, HEAD_DIM), lambda s, h, i: (0, 0)),
    ]
    args = [lam, z, z, z, bias, g]
    in_specs, args, aliases = _alias_prev(in_specs, args, prev)
    kern = functools.partial(_attn_a_kernel, tq=tq, tk=tk, nk=seq // tk, r=r, post_scale=post_scale)
    return pl.pallas_call(
        kern,
        grid=(nseg, HEADS, nq),
        in_specs=in_specs,
        out_specs=pl.BlockSpec((tq, HEAD_DIM), lambda s, h, i: (qoff + s * nq + i, h)),
        out_shape=jax.ShapeDtypeStruct((t, HEADS * HEAD_DIM), BF16),
        input_output_aliases=aliases,
        compiler_params=_cparams(("parallel", "parallel", "arbitrary")),
        name="attention_a",
    )(*args)


def _merge_kernel(oa_ref, ob_ref, oc_ref, wa_ref, wb_ref, wc_ref, ga_ref, gb_ref, gc_ref, o_ref):
    def branch(o_r, w_r, g_r):
        y = jnp.dot(o_r[...], w_r[...], preferred_element_type=F32)
        return jax.nn.sigmoid(g_r[...].astype(F32)) * y

    merged = branch(oa_ref, wa_ref, ga_ref) + branch(ob_ref, wb_ref, gb_ref) + branch(oc_ref, wc_ref, gc_ref)
    o_ref[...] = merged.astype(o_ref.dtype)


def gated_merge(oa, ob, oc, wa, wb, wc, z, tm, tn):
    t, k = oa.shape
    n = wa.shape[1]
    o_spec = pl.BlockSpec((tm, k), lambda i, j: (i, 0))
    w_spec = pl.BlockSpec((k, tn), lambda i, j: (0, j))
    gate = lambda b: pl.BlockSpec((tm, tn), lambda i, j: (i, (Z_GATES + b * n) // tn + j))
    return pl.pallas_call(
        _merge_kernel,
        grid=(t // tm, n // tn),
        in_specs=[o_spec, o_spec, o_spec, w_spec, w_spec, w_spec, gate(0), gate(1), gate(2)],
        out_specs=pl.BlockSpec((tm, tn), lambda i, j: (i, j)),
        out_shape=jax.ShapeDtypeStruct((t, n), BF16),
        compiler_params=_cparams(("parallel", "arbitrary")),
        name="gated_merge",
    )(oa, ob, oc, wa, wb, wc, z, z, z)


def _mm_postnorm_kernel(a_ref, w_ref, g_ref, x_ref, o_ref):
    y = jnp.dot(a_ref[...], w_ref[...], preferred_element_type=F32)
    o_ref[...] = x_ref[...] + _rms(y, g_ref[...])


def mm_postnorm_residual(a, w, g, x, tm):
    t, k = a.shape
    n = w.shape[1]
    return pl.pallas_call(
        _mm_postnorm_kernel,
        grid=(t // tm,),
        in_specs=[
            pl.BlockSpec((tm, k), lambda i: (i, 0)),
            pl.BlockSpec((k, n), lambda i: (0, 0)),
            pl.BlockSpec((1, n), lambda i: (0, 0)),
            pl.BlockSpec((tm, n), lambda i: (i, 0)),
        ],
        out_specs=pl.BlockSpec((tm, n), lambda i: (i, 0)),
        out_shape=jax.ShapeDtypeStruct((t, n), F32),
        compiler_params=_cparams(("parallel",)),
        name="mm_postnorm_residual",
    )(a, w, g, x)


def _xattn_kernel(x_ref, gpre_ref, wq_ref, kv_ref, wo_ref, gpost_ref, o_ref, *, q_scale):
    x = x_ref[...]
    h = _rms(x, gpre_ref[...]).astype(BF16)
    q = (jnp.dot(h, wq_ref[...], preferred_element_type=F32) * q_scale).astype(BF16)
    kv_cols = X_HEADS * HEAD_DIM
    outs = []
    for hd in range(X_HEADS):
        sl = slice(hd * HEAD_DIM, (hd + 1) * HEAD_DIM)
        kh = kv_ref[0, :, sl]
        vh = kv_ref[0, :, kv_cols + hd * HEAD_DIM:kv_cols + (hd + 1) * HEAD_DIM]
        s = lax.dot_general(q[:, sl], kh, (((1,), (1,)), ((), ())), preferred_element_type=F32)
        p = jnp.exp2(s - jnp.max(s, axis=-1, keepdims=True))
        l = jnp.sum(p, axis=-1, keepdims=True)
        outs.append((jnp.dot(p.astype(BF16), vh, preferred_element_type=F32) / l).astype(BF16))
    o = jnp.concatenate(outs, axis=1)
    y = jnp.dot(o, wo_ref[...], preferred_element_type=F32)
    o_ref[...] = x + _rms(y, gpost_ref[...])


def memory_xattn(x, gpre, wq, memkv, mem_block, wo, gpost, tm):
    t, d = x.shape
    full = lambda a: pl.BlockSpec(a.shape, lambda i: (0,) * a.ndim)
    kern = functools.partial(_xattn_kernel, q_scale=HEAD_DIM ** -0.5 * LOG2E)
    return pl.pallas_call(
        kern,
        grid=(t // tm,),
        in_specs=[
            pl.BlockSpec((tm, d), lambda i: (i, 0)),
            full(gpre), full(wq),
            pl.BlockSpec((1,) + memkv.shape[1:], lambda i: (mem_block(i), 0, 0)),
            full(wo), full(gpost),
        ],
        out_specs=pl.BlockSpec((tm, d), lambda i: (i, 0)),
        out_shape=jax.ShapeDtypeStruct((t, d), F32),
        compiler_params=_cparams(("parallel",)),
        name="memory_xattn",
    )(x, gpre, wq, memkv, wo, gpost)


def _ffn_kernel(x_ref, gpre_ref, wg_ref, wu_ref, wd_ref, gpost_ref, o_ref, h_ref, acc_ref):
    f = pl.program_id(1)

    @pl.when(f == 0)
    def _():
        h_ref[...] = _rms(x_ref[...], gpre_ref[...]).astype(BF16)
        acc_ref[...] = jnp.zeros_like(acc_ref)

    h = h_ref[...]
    a = jnp.dot(h, wg_ref[...], preferred_element_type=F32)
    b = jnp.dot(h, wu_ref[...], preferred_element_type=F32)
    t = (a * jax.nn.sigmoid(a) * b).astype(BF16)
    acc_ref[...] += jnp.dot(t, wd_ref[...], preferred_element_type=F32)

    @pl.when(f == pl.num_programs(1) - 1)
    def _():
        o_ref[...] = x_ref[...] + _rms(acc_ref[...], gpost_ref[...])


def swiglu_ffn(x, gpre, wg, wu, wd, gpost, tm, tf):
    t, d = x.shape
    ff = wg.shape[1]
    return pl.pallas_call(
        _ffn_kernel,
        grid=(t // tm, ff // tf),
        in_specs=[
            pl.BlockSpec((tm, d), lambda i, f: (i, 0)),
            pl.BlockSpec((1, d), lambda i, f: (0, 0)),
            pl.BlockSpec((d, tf), lambda i, f: (0, f)),
            pl.BlockSpec((d, tf), lambda i, f: (0, f)),
            pl.BlockSpec((tf, d), lambda i, f: (f, 0)),
            pl.BlockSpec((1, d), lambda i, f: (0, 0)),
        ],
        out_specs=pl.BlockSpec((tm, d), lambda i, f: (i, 0)),
        out_shape=jax.ShapeDtypeStruct((t, d), F32),
        scratch_shapes=[pltpu.VMEM((tm, d), BF16), pltpu.VMEM((tm, d), F32)],
        compiler_params=_cparams(("parallel", "arbitrary")),
        name="swiglu_ffn",
    )(x, gpre, wg, wu, wd, gpost)


def _rotary_tables(seq):
    pos = jnp.arange(seq)

    def cs(p):
        inv = ROPE_THETA ** (-jnp.arange(0, 64, 2, dtype=F32) / 64)
        ang = p.astype(F32)[:, None] * inv[None, :]
        c, s = jnp.cos(ang), jnp.sin(ang)
        return jnp.concatenate([c, c], axis=-1), jnp.concatenate([-s, s], axis=-1)

    c_row, s_row = cs(pos // GRID_W)
    c_col, s_col = cs(pos % GRID_W)
    c_tok, s_tok = cs(pos)
    return (jnp.concatenate([c_row, c_col], axis=-1), jnp.concatenate([s_row, s_col], axis=-1),
            jnp.concatenate([c_tok, c_tok], axis=-1), jnp.concatenate([s_tok, s_tok], axis=-1))


def _t5_bucket(rel):
    nb = NUM_BUCKETS // 2
    max_exact = nb // 2
    ret = (rel > 0).astype(jnp.int32) * nb
    n = jnp.abs(rel)
    nf = jnp.maximum(n, 1).astype(F32)
    large = max_exact + (jnp.log(nf / max_exact) / math.log(MAX_DISTANCE / max_exact)
                         * (nb - max_exact)).astype(jnp.int32)
    large = jnp.minimum(large, nb - 1)
    return ret + jnp.where(n < max_exact, n, large)


def _bias_tiles(rel_bias, tq, tk):
    assert tq >= 128 and tk % tq == 0
    r = tk // tq
    o = jnp.arange(-r - 1, 3)
    rel = o[:, None, None] * tq + jnp.arange(tk)[None, None, :] - jnp.arange(tq)[None, :, None]
    return rel_bias.T.astype(F32)[:, _t5_bucket(rel)] * LOG2E


def kernel(x_prompt, x_sample, mem_prompt, mem_sample, rel_bias, g_mix_pre, g_mix_post, w_in, lam_q1, lam_k1, lam_q2, lam_k2, g_a_out, g_b_q, g_b_k, g_c_q, g_c_kv, w_c_q_up, w_c_kv_up, w_br_a, w_br_b, w_br_c, w_mix_out, g_x_pre, g_x_post, g_mem, w_x_q, w_x_kv, w_x_out, g_ffn_pre, g_ffn_post, w_ffn_gate, w_ffn_up, w_ffn_down):
    depth = w_in.shape[0]
    pb, ps, d = x_prompt.shape
    sb, ss, _ = x_sample.shape
    tp, ts = pb * ps, sb * ss
    t = tp + ts
    groups = [(0, ps, pb), (tp, ss, sb)]

    tm_prep = 512
    tm_x = 512
    tq_a, tk_a = 256, 512
    tq, tk = 512, 512

    x = jnp.concatenate([x_prompt.reshape(tp, d), x_sample.reshape(ts, d)], axis=0)
    mem = jnp.concatenate([mem_prompt.reshape(pb * MEM_TOKENS, d), mem_sample.reshape(sb * MEM_TOKENS, d)], axis=0)

    tables = _rotary_tables(max(ps, ss))

    def pos_block(i, tm=tm_prep):
        return jnp.where(i < tp // tm, i % (ps // tm), (i - tp // tm) % (ss // tm))

    def mem_block(i, tm=tm_x):
        return jnp.where(i < tp // tm, i // (ps // tm), pb + (i - tp // tm) // (ss // tm))

    bias = _bias_tiles(rel_bias, tq_a, tk_a)

    row = lambda g: g.reshape(1, -1).astype(F32)
    w_in_p = jnp.concatenate(
        [w_in[:, :, :Z_RAW_SPLIT].astype(BF16), jnp.zeros((depth, d, Z_GATES - Z_RAW_SPLIT), BF16),
         w_in[:, :, Z_RAW_SPLIT:].astype(BF16)], axis=-1)
    in_scale = jnp.concatenate([jnp.full((1, Z_AK), A_QK_DIM ** -0.5 * LOG2E, F32),
                                jnp.ones((1, Z_COLS - Z_AK), F32)], axis=-1)
    wq3 = w_c_q_up.reshape(depth, C_Q_RANK, HEADS, C_NOPE + C_ROPE)
    w_cq = jnp.concatenate(
        [wq3[..., :C_NOPE].reshape(depth, C_Q_RANK, HEADS * C_NOPE),
         jnp.pad(wq3[..., C_NOPE:], ((0, 0), (0, 0), (0, 0), (0, HEAD_DIM - C_ROPE))).reshape(
             depth, C_Q_RANK, HEADS * HEAD_DIM)], axis=-1).astype(BF16)
    w_ckv = w_c_kv_up.astype(BF16)
    w_a, w_b, w_c = w_br_a.astype(BF16), w_br_b.astype(BF16), w_br_c.astype(BF16)
    w_mix = w_mix_out.astype(BF16)
    w_xq, w_xkv, w_xo = w_x_q.astype(BF16), w_x_kv.astype(BF16), w_x_out.astype(BF16)
    w_fg, w_fu, w_fd = w_ffn_gate.astype(BF16), w_ffn_up.astype(BF16), w_ffn_down.astype(BF16)
    ones_kv = jnp.ones((1, w_x_kv.shape[-1]), F32)

    for l in range(depth):
        lam_init = 0.8 - 0.6 * math.exp(-0.3 * l)
        lam = (jnp.exp(jnp.sum(lam_q1[l].astype(F32) * lam_k1[l].astype(F32)))
               - jnp.exp(jnp.sum(lam_q2[l].astype(F32) * lam_k2[l].astype(F32))) + lam_init).reshape(1)

        z = rms_matmul(x, row(g_mix_pre[l]), w_in_p[l], in_scale, tm=1024, tn=512)
        qb, kb, cq, ckv, kr = mixer_prep(z, tables, pos_block, row(g_b_q[l]), row(g_b_k[l]), row(g_c_q[l]),
                                         row(g_c_kv[l]), w_cq[l], w_ckv[l], tm_prep)

        oa = ob = oc = None
        for row_off, seq, nseg in groups:
            oa = attention_a(lam, z, bias, row(g_a_out[l]), seq, row_off, nseg, tq_a, tk_a,
                             1.0 - lam_init, oa)
            ob = attention(
                [(qb, lambda h: h)],
                [(kb, lambda h: h // (HEADS // B_KV_HEADS))],
                (z, lambda h: Z_BV // HEAD_DIM + h // (HEADS // B_KV_HEADS)),
                t, seq, row_off, nseg, tq, tk, ob)
            oc = attention(
                [(cq, lambda h: h), (cq, lambda h: HEADS + h)],
                [(ckv, lambda h: 2 * h), (kr, lambda h: 0)],
                (ckv, lambda h: 2 * h + 1),
                t, seq, row_off, nseg, tq, tk, oc)

        merged = gated_merge(oa, ob, oc, w_a[l], w_b[l], w_c[l], z, tm=1024, tn=512)
        x = mm_postnorm_residual(merged, w_mix[l], row(g_mix_post[l]), x, tm=512)

        memkv = rms_matmul(mem, row(g_mem[l]), w_xkv[l], ones_kv, tm=mem.shape[0], tn=512)
        memkv = memkv.reshape(pb + sb, MEM_TOKENS, -1)
        x = memory_xattn(x, row(g_x_pre[l]), w_xq[l], memkv, mem_block, w_xo[l], row(g_x_post[l]), tm_x)

        x = swiglu_ffn(x, row(g_ffn_pre[l]), w_fg[l], w_fu[l], w_fd[l], row(g_ffn_post[l]), tm=512, tf=512)

    return (x[:tp].reshape(pb, ps, d), x[tp:].reshape(sb, ss, d))
```

```python
import functools
import math

import jax
import jax.numpy as jnp
from jax import lax
from jax.experimental import pallas as pl
from jax.experimental.pallas import tpu as pltpu

F32 = jnp.float32
BF16 = jnp.bfloat16

EPS = 1e-6
ROPE_THETA = 10000.0
LOG2E = 1.4426950408889634
GRID_W = 64
NUM_BUCKETS = 32
MAX_DISTANCE = 128

D_MODEL = 2048
HEADS = 8
HEAD_DIM = 128
A_QK_DIM = 64
B_KV_HEADS = 2
C_Q_RANK = 512
C_KV_RANK = 256
C_NOPE = 128
C_ROPE = 64
X_HEADS = 4
MEM_TOKENS = 256

Z_AQ, Z_AK, Z_AV, Z_BQ, Z_BK, Z_BV, Z_CQA, Z_CKVA, Z_CKR = 0, 1024, 2048, 3072, 4096, 4352, 4608, 5120, 5376
Z_RAW_SPLIT = 5440
Z_GATES = 5632
Z_COLS = Z_GATES + 3 * D_MODEL

VMEM_LIMIT = 56 * 1024 * 1024


def _cparams(sem):
    return pltpu.CompilerParams(dimension_semantics=sem, vmem_limit_bytes=VMEM_LIMIT)


def _rms(x, g):
    return x * lax.rsqrt(jnp.mean(x * x, axis=-1, keepdims=True) + EPS) * g


def _swap32(x):
    lane = lax.broadcasted_iota(jnp.int32, x.shape, 1)
    return jnp.where((lane % 64) < 32, pltpu.roll(x, 96, 1), pltpu.roll(x, 32, 1))


def _rope(x, cos, sin_signed):
    return x * cos + _swap32(x) * sin_signed


def _rms_matmul_kernel(x_ref, g_ref, w_ref, cs_ref, o_ref, xn_ref):
    @pl.when(pl.program_id(1) == 0)
    def _():
        xn_ref[...] = _rms(x_ref[...].astype(F32), g_ref[...]).astype(BF16)

    acc = jnp.dot(xn_ref[...], w_ref[...], preferred_element_type=F32)
    o_ref[...] = (acc * cs_ref[...]).astype(o_ref.dtype)


def rms_matmul(x, g, w, colscale, tm, tn, out_dtype=BF16):
    m, k = x.shape
    n = w.shape[1]
    return pl.pallas_call(
        _rms_matmul_kernel,
        grid=(m // tm, n // tn),
        in_specs=[
            pl.BlockSpec((tm, k), lambda i, j: (i, 0)),
            pl.BlockSpec((1, k), lambda i, j: (0, 0)),
            pl.BlockSpec((k, tn), lambda i, j: (0, j)),
            pl.BlockSpec((1, tn), lambda i, j: (0, j)),
        ],
        out_specs=pl.BlockSpec((tm, tn), lambda i, j: (i, j)),
        out_shape=jax.ShapeDtypeStruct((m, n), out_dtype),
        scratch_shapes=[pltpu.VMEM((tm, k), BF16)],
        compiler_params=_cparams(("parallel", "arbitrary")),
        name="rms_matmul",
    )(x, g, w, colscale)


def _prep_kernel(bq_ref, bkv_ref, cqa_ref, ckk_ref, cos_a_ref, sin_a_ref, cos_r_ref, sin_r_ref,
                 gbq_ref, gbk_ref, gcq_ref, gckv_ref, wq_ref, wkv_ref,
                 qb_out, kb_out, cq_out, ckv_out, kr_out, *, b_scale, c_scale):
    cos_a, sin_a = cos_a_ref[...], sin_a_ref[...]
    cos_r, sin_r = cos_r_ref[...], sin_r_ref[...]

    for h in range(HEADS):
        sl = slice(h * HEAD_DIM, (h + 1) * HEAD_DIM)
        y = _rms(bq_ref[:, sl].astype(F32), gbq_ref[...])
        qb_out[:, sl] = (_rope(y, cos_a, sin_a) * b_scale).astype(BF16)
    for n in range(B_KV_HEADS):
        sl = slice(n * HEAD_DIM, (n + 1) * HEAD_DIM)
        y = _rms(bkv_ref[:, sl].astype(F32), gbk_ref[...])
        kb_out[:, sl] = _rope(y, cos_a, sin_a).astype(BF16)

    xq = _rms(cqa_ref[...].astype(F32), gcq_ref[...]).astype(BF16)
    cq = jnp.dot(xq, wq_ref[...], preferred_element_type=F32)
    nope_cols = HEADS * C_NOPE
    cq_out[:, :nope_cols] = (cq[:, :nope_cols] * c_scale).astype(BF16)
    for h in range(HEADS):
        sl = slice(nope_cols + h * HEAD_DIM, nope_cols + (h + 1) * HEAD_DIM)
        cq_out[:, sl] = (_rope(cq[:, sl], cos_r, sin_r) * c_scale).astype(BF16)

    xkv = _rms(ckk_ref[:, :C_KV_RANK].astype(F32), gckv_ref[...]).astype(BF16)
    ckv_out[...] = jnp.dot(xkv, wkv_ref[...], preferred_element_type=F32).astype(BF16)
    kr = ckk_ref[:, C_KV_RANK:C_KV_RANK + HEAD_DIM].astype(F32)
    kr_out[...] = _rope(kr, cos_r, sin_r).astype(BF16)


def mixer_prep(z, tables, pos_block, gbq, gbk, gcq, gckv, wq, wkv, tm):
    t = z.shape[0]
    cos_a, sin_a, cos_r, sin_r = tables
    row = lambda w: pl.BlockSpec((tm, w), lambda i: (i, 0))
    tab = pl.BlockSpec((tm, HEAD_DIM), lambda i: (pos_block(i), 0))
    full = lambda a: pl.BlockSpec(a.shape, lambda i: (0,) * a.ndim)
    kern = functools.partial(_prep_kernel, b_scale=HEAD_DIM ** -0.5 * LOG2E,
                             c_scale=(C_NOPE + C_ROPE) ** -0.5 * LOG2E)
    return pl.pallas_call(
        kern,
        grid=(t // tm,),
        in_specs=[
            pl.BlockSpec((tm, 1024), lambda i: (i, Z_BQ // 1024)),
            pl.BlockSpec((tm, 512), lambda i: (i, Z_BK // 512)),
            pl.BlockSpec((tm, 512), lambda i: (i, Z_CQA // 512)),
            pl.BlockSpec((tm, 512), lambda i: (i, Z_CKVA // 512)),
            tab, tab, tab, tab,
            full(gbq), full(gbk), full(gcq), full(gckv), full(wq), full(wkv),
        ],
        out_specs=[row(1024), row(256), row(2048), row(2048), row(HEAD_DIM)],
        out_shape=[
            jax.ShapeDtypeStruct((t, 1024), BF16),
            jax.ShapeDtypeStruct((t, 256), BF16),
            jax.ShapeDtypeStruct((t, 2048), BF16),
            jax.ShapeDtypeStruct((t, 2048), BF16),
            jax.ShapeDtypeStruct((t, HEAD_DIM), BF16),
        ],
        compiler_params=_cparams(("parallel",)),
        name="mixer_prep",
    )(z, z, z, z, cos_a, sin_a, cos_r, sin_r, gbq, gbk, gcq, gckv, wq, wkv)


def _flash(q, k_refs, v_ref, nk, tk, bias_fn=None):
    m_rows = q.shape[0]

    def body(i, carry):
        m, l, acc = carry
        off = pl.multiple_of(i * tk, tk)
        ks = [r[pl.ds(off, tk), :] for r in k_refs]
        k = ks[0] if len(ks) == 1 else jnp.concatenate(ks, axis=1)
        s = lax.dot_general(q, k, (((1,), (1,)), ((), ())), preferred_element_type=F32)
        if bias_fn is not None:
            s = bias_fn(i, s)
        m_new = jnp.maximum(m, jnp.max(s, axis=-1, keepdims=True))
        p = jnp.exp2(s - m_new)
        alpha = jnp.exp2(m - m_new)
        l = alpha * l + jnp.sum(p, axis=-1, keepdims=True)
        pv = jnp.dot(p.astype(BF16), v_ref[pl.ds(off, tk), :], preferred_element_type=F32)
        return m_new, l, alpha * acc + pv

    init = (jnp.full((m_rows, 1), -jnp.inf, F32), jnp.zeros((m_rows, 1), F32),
            jnp.zeros((m_rows, v_ref.shape[1]), F32))
    return lax.fori_loop(0, nk, body, init, unroll=4)


def _attn_kernel(*refs, n_q, n_k, nk, tk):
    q_refs, k_refs = refs[:n_q], refs[n_q:n_q + n_k]
    v_ref, o_ref = refs[n_q + n_k], refs[-1]
    qs = [r[...] for r in q_refs]
    q = qs[0] if n_q == 1 else jnp.concatenate(qs, axis=1)
    _, l, acc = _flash(q, k_refs, v_ref, nk, tk)
    o_ref[...] = (acc / l).astype(o_ref.dtype)


def _attn_a_kernel(lam_ref, q_ref, k_ref, v_ref, bias_ref, g_ref, *rest, tq, tk, nk, r, post_scale):
    o_ref = rest[-1]
    qi = pl.program_id(2)
    q = q_ref[...]
    lane = lax.broadcasted_iota(jnp.int32, q.shape, 1)
    zero = jnp.zeros_like(q)
    q2 = jnp.concatenate([jnp.where(lane < A_QK_DIM, q, zero), jnp.where(lane >= A_QK_DIM, q, zero)], axis=0)

    def bias_fn(i, s):
        idx = jnp.clip(i * r - qi, -r - 1, 2) + r + 1
        b = bias_ref[0, idx]
        return jnp.concatenate([s[:tq] + b, s[tq:] + b], axis=0)

    _, l, acc = _flash(q2, [k_ref], v_ref, nk, tk, bias_fn)
    o = acc / l
    d = o[:tq] - lam_ref[0] * o[tq:]
    o_ref[...] = (_rms(d, g_ref[...]) * post_scale).astype(o_ref.dtype)


def _alias_prev(in_specs, args, prev):
    if prev is None:
        return in_specs, args, {}
    return in_specs + [pl.BlockSpec(memory_space=pl.ANY)], args + [prev], {len(args): 0}


def attention(q_parts, k_parts, v_part, t, seq, row_off, nseg, tq, tk, prev):
    nq = seq // tq
    qoff, koff = row_off // tq, row_off // seq
    in_specs, args = [], []
    for a, cf in q_parts:
        in_specs.append(pl.BlockSpec((tq, HEAD_DIM), lambda s, h, i, cf=cf: (qoff + s * nq + i, cf(h))))
        args.append(a)
    for a, cf in list(k_parts) + [v_part]:
        in_specs.append(pl.BlockSpec((seq, HEAD_DIM), lambda s, h, i, cf=cf: (koff + s, cf(h))))
        args.append(a)
    in_specs, args, aliases = _alias_prev(in_specs, args, prev)
    kern = functools.partial(_attn_kernel, n_q=len(q_parts), n_k=len(k_parts), nk=seq // tk, tk=tk)
    return pl.pallas_call(
        kern,
        grid=(nseg, HEADS, nq),
        in_specs=in_specs,
        out_specs=pl.BlockSpec((tq, HEAD_DIM), lambda s, h, i: (qoff + s * nq + i, h)),
        out_shape=jax.ShapeDtypeStruct((t, HEADS * HEAD_DIM), BF16),
        input_output_aliases=aliases,
        compiler_params=_cparams(("parallel", "parallel", "arbitrary")),
        name="attention",
    )(*args)


def attention_a(lam, z, bias, g, seq, row_off, nseg, tq, tk, post_scale, prev):
    t = z.shape[0]
    nq = seq // tq
    r = tk // tq
    qoff, koff = row_off // tq, row_off // seq
    in_specs = [
        pl.BlockSpec(memory_space=pltpu.SMEM),
        pl.BlockSpec((tq, HEAD_DIM), lambda s, h, i: (qoff + s * nq + i, Z_AQ // HEAD_DIM + h)),
        pl.BlockSpec((seq, HEAD_DIM), lambda s, h, i: (koff + s, Z_AK // HEAD_DIM + h)),
        pl.BlockSpec((seq, HEAD_DIM), lambda s, h, i: (koff + s, Z_AV // HEAD_DIM + h)),
        pl.BlockSpec((1, r + 4, tq, tk), lambda s, h, i: (h, 0, 0, 0)),
        pl.BlockSpec((1, HEAD_DIM), lambda s, h, i: (0, 0)),
    ]
    args = [lam, z, z, z, bias, g]
    in_specs, args, aliases = _alias_prev(in_specs, args, prev)
    kern = functools.partial(_attn_a_kernel, tq=tq, tk=tk, nk=seq // tk, r=r, post_scale=post_scale)
    return pl.pallas_call(
        kern,
        grid=(nseg, HEADS, nq),
        in_specs=in_specs,
        out_specs=pl.BlockSpec((tq, HEAD_DIM), lambda s, h, i: (qoff + s * nq + i, h)),
        out_shape=jax.ShapeDtypeStruct((t, HEADS * HEAD_DIM), BF16),
        input_output_aliases=aliases,
        compiler_params=_cparams(("parallel", "parallel", "arbitrary")),
        name="attention_a",
    )(*args)


def _merge_kernel(oa_ref, ob_ref, oc_ref, wa_ref, wb_ref, wc_ref, ga_ref, gb_ref, gc_ref, o_ref):
    def branch(o_r, w_r, g_r):
        y = jnp.dot(o_r[...], w_r[...], preferred_element_type=F32)
        return jax.nn.sigmoid(g_r[...].astype(F32)) * y

    merged = branch(oa_ref, wa_ref, ga_ref) + branch(ob_ref, wb_ref, gb_ref) + branch(oc_ref, wc_ref, gc_ref)
    o_ref[...] = merged.astype(o_ref.dtype)


def gated_merge(oa, ob, oc, wa, wb, wc, z, tm, tn):
    t, k = oa.shape
    n = wa.shape[1]
    o_spec = pl.BlockSpec((tm, k), lambda i, j: (i, 0))
    w_spec = pl.BlockSpec((k, tn), lambda i, j: (0, j))
    gate = lambda b: pl.BlockSpec((tm, tn), lambda i, j: (i, (Z_GATES + b * n) // tn + j))
    return pl.pallas_call(
        _merge_kernel,
        grid=(t // tm, n // tn),
        in_specs=[o_spec, o_spec, o_spec, w_spec, w_spec, w_spec, gate(0), gate(1), gate(2)],
        out_specs=pl.BlockSpec((tm, tn), lambda i, j: (i, j)),
        out_shape=jax.ShapeDtypeStruct((t, n), BF16),
        compiler_params=_cparams(("parallel", "arbitrary")),
        name="gated_merge",
    )(oa, ob, oc, wa, wb, wc, z, z, z)


def _mm_postnorm_kernel(a_ref, w_ref, g_ref, x_ref, o_ref):
    y = jnp.dot(a_ref[...], w_ref[...], preferred_element_type=F32)
    o_ref[...] = x_ref[...] + _rms(y, g_ref[...])


def mm_postnorm_residual(a, w, g, x, tm):
    t, k = a.shape
    n = w.shape[1]
    return pl.pallas_call(
        _mm_postnorm_kernel,
        grid=(t // tm,),
        in_specs=[
            pl.BlockSpec((tm, k), lambda i: (i, 0)),
            pl.BlockSpec((k, n), lambda i: (0, 0)),
            pl.BlockSpec((1, n), lambda i: (0, 0)),
            pl.BlockSpec((tm, n), lambda i: (i, 0)),
        ],
        out_specs=pl.BlockSpec((tm, n), lambda i: (i, 0)),
        out_shape=jax.ShapeDtypeStruct((t, n), F32),
        compiler_params=_cparams(("parallel",)),
        name="mm_postnorm_residual",
    )(a, w, g, x)


def _xattn_kernel(x_ref, gpre_ref, wq_ref, kv_ref, wo_ref, gpost_ref, o_ref, *, q_scale):
    x = x_ref[...]
    h = _rms(x, gpre_ref[...]).astype(BF16)
    q = (jnp.dot(h, wq_ref[...], preferred_element_type=F32) * q_scale).astype(BF16)
    kv_cols = X_HEADS * HEAD_DIM
    outs = []
    for hd in range(X_HEADS):
        sl = slice(hd * HEAD_DIM, (hd + 1) * HEAD_DIM)
        kh = kv_ref[0, :, sl]
        vh = kv_ref[0, :, kv_cols + hd * HEAD_DIM:kv_cols + (hd + 1) * HEAD_DIM]
        s = lax.dot_general(q[:, sl], kh, (((1,), (1,)), ((), ())), preferred_element_type=F32)
        p = jnp.exp2(s - jnp.max(s, axis=-1, keepdims=True))
        l = jnp.sum(p, axis=-1, keepdims=True)
        outs.append((jnp.dot(p.astype(BF16), vh, preferred_element_type=F32) / l).astype(BF16))
    o = jnp.concatenate(outs, axis=1)
    y = jnp.dot(o, wo_ref[...], preferred_element_type=F32)
    o_ref[...] = x + _rms(y, gpost_ref[...])


def memory_xattn(x, gpre, wq, memkv, mem_block, wo, gpost, tm):
    t, d = x.shape
    full = lambda a: pl.BlockSpec(a.shape, lambda i: (0,) * a.ndim)
    kern = functools.partial(_xattn_kernel, q_scale=HEAD_DIM ** -0.5 * LOG2E)
    return pl.pallas_call(
        kern,
        grid=(t // tm,),
        in_specs=[
            pl.BlockSpec((tm, d), lambda i: (i, 0)),
            full(gpre), full(wq),
            pl.BlockSpec((1,) + memkv.shape[1:], lambda i: (mem_block(i), 0, 0)),
            full(wo), full(gpost),
        ],
        out_specs=pl.BlockSpec((tm, d), lambda i: (i, 0)),
        out_shape=jax.ShapeDtypeStruct((t, d), F32),
        compiler_params=_cparams(("parallel",)),
        name="memory_xattn",
    )(x, gpre, wq, memkv, wo, gpost)


def _ffn_kernel(x_ref, gpre_ref, wg_ref, wu_ref, wd_ref, gpost_ref, o_ref, h_ref, acc_ref):
    f = pl.program_id(1)

    @pl.when(f == 0)
    def _():
        h_ref[...] = _rms(x_ref[...], gpre_ref[...]).astype(BF16)
        acc_ref[...] = jnp.zeros_like(acc_ref)

    h = h_ref[...]
    a = jnp.dot(h, wg_ref[...], preferred_element_type=F32)
    b = jnp.dot(h, wu_ref[...], preferred_element_type=F32)
    t = (a * jax.nn.sigmoid(a) * b).astype(BF16)
    acc_ref[...] += jnp.dot(t, wd_ref[...], preferred_element_type=F32)

    @pl.when(f == pl.num_programs(1) - 1)
    def _():
        o_ref[...] = x_ref[...] + _rms(acc_ref[...], gpost_ref[...])


def swiglu_ffn(x, gpre, wg, wu, wd, gpost, tm, tf):
    t, d = x.shape
    ff = wg.shape[1]
    return pl.pallas_call(
        _ffn_kernel,
        grid=(t // tm, ff // tf),
        in_specs=[
            pl.BlockSpec((tm, d), lambda i, f: (i, 0)),
            pl.BlockSpec((1, d), lambda i, f: (0, 0)),
            pl.BlockSpec((d, tf), lambda i, f: (0, f)),
            pl.BlockSpec((d, tf), lambda i, f: (0, f)),
            pl.BlockSpec((tf, d), lambda i, f: (f, 0)),
            pl.BlockSpec((1, d), lambda i, f: (0, 0)),
        ],
        out_specs=pl.BlockSpec((tm, d), lambda i, f: (i, 0)),
        out_shape=jax.ShapeDtypeStruct((t, d), F32),
        scratch_shapes=[pltpu.VMEM((tm, d), BF16), pltpu.VMEM((tm, d), F32)],
        compiler_params=_cparams(("parallel", "arbitrary")),
        name="swiglu_ffn",
    )(x, gpre, wg, wu, wd, gpost)


def _rotary_tables(seq):
    pos = jnp.arange(seq)

    def cs(p):
        inv = ROPE_THETA ** (-jnp.arange(0, 64, 2, dtype=F32) / 64)
        ang = p.astype(F32)[:, None] * inv[None, :]
        c, s = jnp.cos(ang), jnp.sin(ang)
        return jnp.concatenate([c, c], axis=-1), jnp.concatenate([-s, s], axis=-1)

    c_row, s_row = cs(pos // GRID_W)
    c_col, s_col = cs(pos % GRID_W)
    c_tok, s_tok = cs(pos)
    return (jnp.concatenate([c_row, c_col], axis=-1), jnp.concatenate([s_row, s_col], axis=-1),
            jnp.concatenate([c_tok, c_tok], axis=-1), jnp.concatenate([s_tok, s_tok], axis=-1))


def _t5_bucket(rel):
    nb = NUM_BUCKETS // 2
    max_exact = nb // 2
    ret = (rel > 0).astype(jnp.int32) * nb
    n = jnp.abs(rel)
    nf = jnp.maximum(n, 1).astype(F32)
    large = max_exact + (jnp.log(nf / max_exact) / math.log(MAX_DISTANCE / max_exact)
                         * (nb - max_exact)).astype(jnp.int32)
    large = jnp.minimum(large, nb - 1)
    return ret + jnp.where(n < max_exact, n, large)


def _bias_kernel(tab_ref, o_ref, *, tq, tk, r):
    h = pl.program_id(1)
    rel = ((pl.program_id(0) - (r + 1)) * tq + lax.broadcasted_iota(jnp.int32, (tq, tk), 1)
           - lax.broadcasted_iota(jnp.int32, (tq, tk), 0))
    bucket = _t5_bucket(rel)
    v = jnp.full((tq, tk), tab_ref[h, 0], F32)
    for b in range(1, NUM_BUCKETS):
        v = jnp.where(bucket == b, tab_ref[h, b], v)
    o_ref[0, 0] = v * LOG2E


def _bias_tiles(rel_bias, tq, tk):
    assert tq >= 128 and tk % tq == 0
    r = tk // tq
    return pl.pallas_call(
        functools.partial(_bias_kernel, tq=tq, tk=tk, r=r),
        grid=(r + 4, HEADS),
        in_specs=[pl.BlockSpec(memory_space=pltpu.SMEM)],
        out_specs=pl.BlockSpec((1, 1, tq, tk), lambda t, h: (h, t, 0, 0)),
        out_shape=jax.ShapeDtypeStruct((HEADS, r + 4, tq, tk), F32),
        compiler_params=_cparams(("parallel", "parallel")),
        name="bias_tiles",
    )(rel_bias.T.astype(F32))


def kernel(x_prompt, x_sample, mem_prompt, mem_sample, rel_bias, g_mix_pre, g_mix_post, w_in, lam_q1, lam_k1, lam_q2, lam_k2, g_a_out, g_b_q, g_b_k, g_c_q, g_c_kv, w_c_q_up, w_c_kv_up, w_br_a, w_br_b, w_br_c, w_mix_out, g_x_pre, g_x_post, g_mem, w_x_q, w_x_kv, w_x_out, g_ffn_pre, g_ffn_post, w_ffn_gate, w_ffn_up, w_ffn_down):
    depth = w_in.shape[0]
    pb, ps, d = x_prompt.shape
    sb, ss, _ = x_sample.shape
    tp, ts = pb * ps, sb * ss
    t = tp + ts
    groups = [(0, ps, pb), (tp, ss, sb)]

    tm_prep = 512
    tm_x = 512
    tq_a, tk_a = 256, 1024
    tq, tk = 512, 1024

    x = jnp.concatenate([x_prompt.reshape(tp, d), x_sample.reshape(ts, d)], axis=0)
    mem = jnp.concatenate([mem_prompt.reshape(pb * MEM_TOKENS, d), mem_sample.reshape(sb * MEM_TOKENS, d)], axis=0)

    tables = _rotary_tables(max(ps, ss))

    def pos_block(i, tm=tm_prep):
        return jnp.where(i < tp // tm, i % (ps // tm), (i - tp // tm) % (ss // tm))

    def mem_block(i, tm=tm_x):
        return jnp.where(i < tp // tm, i // (ps // tm), pb + (i - tp // tm) // (ss // tm))

    bias = _bias_tiles(rel_bias, tq_a, tk_a)

    row = lambda g: g.reshape(1, -1).astype(F32)
    w_in_p = jnp.concatenate(
        [w_in[:, :, :Z_RAW_SPLIT].astype(BF16), jnp.zeros((depth, d, Z_GATES - Z_RAW_SPLIT), BF16),
         w_in[:, :, Z_RAW_SPLIT:].astype(BF16)], axis=-1)
    in_scale = jnp.concatenate([jnp.full((1, Z_AK), A_QK_DIM ** -0.5 * LOG2E, F32),
                                jnp.ones((1, Z_COLS - Z_AK), F32)], axis=-1)
    wq3 = w_c_q_up.reshape(depth, C_Q_RANK, HEADS, C_NOPE + C_ROPE)
    w_cq = jnp.concatenate(
        [wq3[..., :C_NOPE].reshape(depth, C_Q_RANK, HEADS * C_NOPE),
         jnp.pad(wq3[..., C_NOPE:], ((0, 0), (0, 0), (0, 0), (0, HEAD_DIM - C_ROPE))).reshape(
             depth, C_Q_RANK, HEADS * HEAD_DIM)], axis=-1).astype(BF16)
    w_ckv = w_c_kv_up.astype(BF16)
    w_a, w_b, w_c = w_br_a.astype(BF16), w_br_b.astype(BF16), w_br_c.astype(BF16)
    w_mix = w_mix_out.astype(BF16)
    w_xq, w_xkv, w_xo = w_x_q.astype(BF16), w_x_kv.astype(BF16), w_x_out.astype(BF16)
    w_fg, w_fu, w_fd = w_ffn_gate.astype(BF16), w_ffn_up.astype(BF16), w_ffn_down.astype(BF16)
    ones_kv = jnp.ones((1, w_x_kv.shape[-1]), F32)

    for l in range(depth):
        lam_init = 0.8 - 0.6 * math.exp(-0.3 * l)
        lam = (jnp.exp(jnp.sum(lam_q1[l].astype(F32) * lam_k1[l].astype(F32)))
               - jnp.exp(jnp.sum(lam_q2[l].astype(F32) * lam_k2[l].astype(F32))) + lam_init).reshape(1)

        z = rms_matmul(x, row(g_mix_pre[l]), w_in_p[l], in_scale, tm=1024, tn=512)
        qb, kb, cq, ckv, kr = mixer_prep(z, tables, pos_block, row(g_b_q[l]), row(g_b_k[l]), row(g_c_q[l]),
                                         row(g_c_kv[l]), w_cq[l], w_ckv[l], tm_prep)

        oa = ob = oc = None
        for row_off, seq, nseg in groups:
            oa = attention_a(lam, z, bias, row(g_a_out[l]), seq, row_off, nseg, tq_a, tk_a,
                             1.0 - lam_init, oa)
            ob = attention(
                [(qb, lambda h: h)],
                [(kb, lambda h: h // (HEADS // B_KV_HEADS))],
                (z, lambda h: Z_BV // HEAD_DIM + h // (HEADS // B_KV_HEADS)),
                t, seq, row_off, nseg, tq, tk, ob)
            oc = attention(
                [(cq, lambda h: h), (cq, lambda h: HEADS + h)],
                [(ckv, lambda h: 2 * h), (kr, lambda h: 0)],
                (ckv, lambda h: 2 * h + 1),
                t, seq, row_off, nseg, tq, tk, oc)

        merged = gated_merge(oa, ob, oc, w_a[l], w_b[l], w_c[l], z, tm=1024, tn=512)
        x = mm_postnorm_residual(merged, w_mix[l], row(g_mix_post[l]), x, tm=512)

        memkv = rms_matmul(mem, row(g_mem[l]), w_xkv[l], ones_kv, tm=mem.shape[0], tn=512)
        memkv = memkv.reshape(pb + sb, MEM_TOKENS, -1)
        x = memory_xattn(x, row(g_x_pre[l]), w_xq[l], memkv, mem_block, w_xo[l], row(g_x_post[l]), tm_x)

        x = swiglu_ffn(x, row(g_ffn_pre[l]), w_fg[l], w_fu[l], w_fd[l], row(g_ffn_post[l]), tm=512, tf=512)

    return (x[:tp].reshape(pb, ps, d), x[tp:].reshape(sb, ss, d))
```

```python
import functools
import math

import jax
import jax.numpy as jnp
from jax import lax
from jax.experimental import pallas as pl
from jax.experimental.pallas import tpu as pltpu

F32 = jnp.float32
BF16 = jnp.bfloat16

EPS = 1e-6
ROPE_THETA = 10000.0
LOG2E = 1.4426950408889634
GRID_W = 64
NUM_BUCKETS = 32
MAX_DISTANCE = 128

D_MODEL = 2048
HEADS = 8
HEAD_DIM = 128
A_QK_DIM = 64
B_KV_HEADS = 2
C_Q_RANK = 512
C_KV_RANK = 256
C_NOPE = 128
C_ROPE = 64
X_HEADS = 4
MEM_TOKENS = 256

Z_AQ, Z_AK, Z_AV, Z_BQ, Z_BK, Z_BV, Z_CQA, Z_CKVA, Z_CKR = 0, 1024, 2048, 3072, 4096, 4352, 4608, 5120, 5376
Z_RAW_SPLIT = 5440
Z_GATES = 5632
Z_COLS = Z_GATES + 3 * D_MODEL

VMEM_LIMIT = 56 * 1024 * 1024


def _cparams(sem):
    return pltpu.CompilerParams(dimension_semantics=sem, vmem_limit_bytes=VMEM_LIMIT)


def _rms(x, g):
    return x * lax.rsqrt(jnp.mean(x * x, axis=-1, keepdims=True) + EPS) * g


def _swap32(x):
    lane = lax.broadcasted_iota(jnp.int32, x.shape, 1)
    return jnp.where((lane % 64) < 32, pltpu.roll(x, 96, 1), pltpu.roll(x, 32, 1))


def _rope(x, cos, sin_signed):
    return x * cos + _swap32(x) * sin_signed


def _rms_matmul_kernel(x_ref, g_ref, w_ref, cs_ref, o_ref, xn_ref):
    @pl.when(pl.program_id(1) == 0)
    def _():
        xn_ref[...] = _rms(x_ref[...].astype(F32), g_ref[...]).astype(BF16)

    acc = jnp.dot(xn_ref[...], w_ref[...], preferred_element_type=F32)
    o_ref[...] = (acc * cs_ref[...]).astype(o_ref.dtype)


def rms_matmul(x, g, w, colscale, tm, tn, out_dtype=BF16):
    m, k = x.shape
    n = w.shape[1]
    return pl.pallas_call(
        _rms_matmul_kernel,
        grid=(m // tm, n // tn),
        in_specs=[
            pl.BlockSpec((tm, k), lambda i, j: (i, 0)),
            pl.BlockSpec((1, k), lambda i, j: (0, 0)),
            pl.BlockSpec((k, tn), lambda i, j: (0, j)),
            pl.BlockSpec((1, tn), lambda i, j: (0, j)),
        ],
        out_specs=pl.BlockSpec((tm, tn), lambda i, j: (i, j)),
        out_shape=jax.ShapeDtypeStruct((m, n), out_dtype),
        scratch_shapes=[pltpu.VMEM((tm, k), BF16)],
        compiler_params=_cparams(("parallel", "arbitrary")),
        name="rms_matmul",
    )(x, g, w, colscale)


def _prep_kernel(bq_ref, bkv_ref, cqa_ref, ckk_ref, cos_a_ref, sin_a_ref, cos_r_ref, sin_r_ref,
                 gbq_ref, gbk_ref, gcq_ref, gckv_ref, wq_ref, wkv_ref,
                 qb_out, kb_out, cq_out, ckv_out, kr_out, *, b_scale, c_scale):
    cos_a, sin_a = cos_a_ref[...], sin_a_ref[...]
    cos_r, sin_r = cos_r_ref[...], sin_r_ref[...]

    for h in range(HEADS):
        sl = slice(h * HEAD_DIM, (h + 1) * HEAD_DIM)
        y = _rms(bq_ref[:, sl].astype(F32), gbq_ref[...])
        qb_out[:, sl] = (_rope(y, cos_a, sin_a) * b_scale).astype(BF16)
    for n in range(B_KV_HEADS):
        sl = slice(n * HEAD_DIM, (n + 1) * HEAD_DIM)
        y = _rms(bkv_ref[:, sl].astype(F32), gbk_ref[...])
        kb_out[:, sl] = _rope(y, cos_a, sin_a).astype(BF16)

    xq = _rms(cqa_ref[...].astype(F32), gcq_ref[...]).astype(BF16)
    cq = jnp.dot(xq, wq_ref[...], preferred_element_type=F32)
    nope_cols = HEADS * C_NOPE
    cq_out[:, :nope_cols] = (cq[:, :nope_cols] * c_scale).astype(BF16)
    for h in range(HEADS):
        sl = slice(nope_cols + h * HEAD_DIM, nope_cols + (h + 1) * HEAD_DIM)
        cq_out[:, sl] = (_rope(cq[:, sl], cos_r, sin_r) * c_scale).astype(BF16)

    xkv = _rms(ckk_ref[:, :C_KV_RANK].astype(F32), gckv_ref[...]).astype(BF16)
    ckv_out[...] = jnp.dot(xkv, wkv_ref[...], preferred_element_type=F32).astype(BF16)
    kr = ckk_ref[:, C_KV_RANK:C_KV_RANK + HEAD_DIM].astype(F32)
    kr_out[...] = _rope(kr, cos_r, sin_r).astype(BF16)


def mixer_prep(z, tables, pos_block, gbq, gbk, gcq, gckv, wq, wkv, tm):
    t = z.shape[0]
    cos_a, sin_a, cos_r, sin_r = tables
    row = lambda w: pl.BlockSpec((tm, w), lambda i: (i, 0))
    tab = pl.BlockSpec((tm, HEAD_DIM), lambda i: (pos_block(i), 0))
    full = lambda a: pl.BlockSpec(a.shape, lambda i: (0,) * a.ndim)
    kern = functools.partial(_prep_kernel, b_scale=HEAD_DIM ** -0.5 * LOG2E,
                             c_scale=(C_NOPE + C_ROPE) ** -0.5 * LOG2E)
    return pl.pallas_call(
        kern,
        grid=(t // tm,),
        in_specs=[
            pl.BlockSpec((tm, 1024), lambda i: (i, Z_BQ // 1024)),
            pl.BlockSpec((tm, 512), lambda i: (i, Z_BK // 512)),
            pl.BlockSpec((tm, 512), lambda i: (i, Z_CQA // 512)),
            pl.BlockSpec((tm, 512), lambda i: (i, Z_CKVA // 512)),
            tab, tab, tab, tab,
            full(gbq), full(gbk), full(gcq), full(gckv), full(wq), full(wkv),
        ],
        out_specs=[row(1024), row(256), row(2048), row(2048), row(HEAD_DIM)],
        out_shape=[
            jax.ShapeDtypeStruct((t, 1024), BF16),
            jax.ShapeDtypeStruct((t, 256), BF16),
            jax.ShapeDtypeStruct((t, 2048), BF16),
            jax.ShapeDtypeStruct((t, 2048), BF16),
            jax.ShapeDtypeStruct((t, HEAD_DIM), BF16),
        ],
        compiler_params=_cparams(("parallel",)),
        name="mixer_prep",
    )(z, z, z, z, cos_a, sin_a, cos_r, sin_r, gbq, gbk, gcq, gckv, wq, wkv)


def _flash(q, k_refs, v_ref, s_refs, nk, tk, bias_fn=None):
    m_rows = q.shape[0]

    def scores(i):
        ks = [r[i * tk:(i + 1) * tk, :] for r in k_refs]
        k = ks[0] if len(ks) == 1 else jnp.concatenate(ks, axis=1)
        s = lax.dot_general(q, k, (((1,), (1,)), ((), ())), preferred_element_type=F32)
        return s if bias_fn is None else bias_fn(i, s)

    def update(i, s, carry):
        m, l, acc = carry
        m_new = jnp.maximum(m, jnp.max(s, axis=-1, keepdims=True))
        p = jnp.exp2(s - m_new)
        alpha = jnp.exp2(m - m_new)
        l = alpha * l + jnp.sum(p, axis=-1, keepdims=True)
        pv = jnp.dot(p.astype(BF16), v_ref[i * tk:(i + 1) * tk, :], preferred_element_type=F32)
        return m_new, l, alpha * acc + pv

    carry = (jnp.full((m_rows, 1), -jnp.inf, F32), jnp.zeros((m_rows, 1), F32),
             jnp.zeros((m_rows, v_ref.shape[1]), F32))
    s_refs[0][...] = scores(0)
    for i in range(nk):
        if i + 1 < nk:
            s_refs[(i + 1) % 2][...] = scores(i + 1)
        carry = update(i, s_refs[i % 2][...], carry)
    return carry


def _attn_kernel(*refs, n_q, n_k, nk, tk):
    q_refs, k_refs = refs[:n_q], refs[n_q:n_q + n_k]
    v_ref, o_ref, s_refs = refs[n_q + n_k], refs[-3], refs[-2:]
    qs = [r[...] for r in q_refs]
    q = qs[0] if n_q == 1 else jnp.concatenate(qs, axis=1)
    _, l, acc = _flash(q, k_refs, v_ref, s_refs, nk, tk)
    o_ref[...] = (acc / l).astype(o_ref.dtype)


def _attn_a_kernel(lam_ref, q_ref, k_ref, v_ref, bias_ref, g_ref, *rest, tq, tk, nk, r, post_scale):
    o_ref, s_refs = rest[-3], rest[-2:]
    qi = pl.program_id(2)
    q = q_ref[...]
    lane = lax.broadcasted_iota(jnp.int32, q.shape, 1)
    zero = jnp.zeros_like(q)
    q2 = jnp.concatenate([jnp.where(lane < A_QK_DIM, q, zero), jnp.where(lane >= A_QK_DIM, q, zero)], axis=0)

    def bias_fn(i, s):
        idx = jnp.clip(i * r - qi, -r - 1, 2) + r + 1
        b = bias_ref[0, idx]
        return jnp.concatenate([s[:tq] + b, s[tq:] + b], axis=0)

    _, l, acc = _flash(q2, [k_ref], v_ref, s_refs, nk, tk, bias_fn)
    o = acc / l
    d = o[:tq] - lam_ref[0] * o[tq:]
    o_ref[...] = (_rms(d, g_ref[...]) * post_scale).astype(o_ref.dtype)


def attention(q_parts, k_parts, v_part, nseq, seq, tq, tk):
    nq = seq // tq
    in_specs, args = [], []
    for a, cf in q_parts:
        in_specs.append(pl.BlockSpec((tq, HEAD_DIM), lambda s, h, i, cf=cf: (s * nq + i, cf(h))))
        args.append(a)
    for a, cf in list(k_parts) + [v_part]:
        in_specs.append(pl.BlockSpec((seq, HEAD_DIM), lambda s, h, i, cf=cf: (s, cf(h))))
        args.append(a)
    kern = functools.partial(_attn_kernel, n_q=len(q_parts), n_k=len(k_parts), nk=seq // tk, tk=tk)
    return pl.pallas_call(
        kern,
        grid=(nseq, HEADS, nq),
        in_specs=in_specs,
        out_specs=pl.BlockSpec((tq, HEAD_DIM), lambda s, h, i: (s * nq + i, h)),
        out_shape=jax.ShapeDtypeStruct((nseq * seq, HEADS * HEAD_DIM), BF16),
        scratch_shapes=[pltpu.VMEM((tq, tk), F32)] * 2,
        compiler_params=_cparams(("parallel", "parallel", "arbitrary")),
        name="attention",
    )(*args)


def attention_a(lam, z, bias, g, nseq, seq, tq, tk, post_scale):
    nq = seq // tq
    r = tk // tq
    in_specs = [
        pl.BlockSpec(memory_space=pltpu.SMEM),
        pl.BlockSpec((tq, HEAD_DIM), lambda s, h, i: (s * nq + i, Z_AQ // HEAD_DIM + h)),
        pl.BlockSpec((seq, HEAD_DIM), lambda s, h, i: (s, Z_AK // HEAD_DIM + h)),
        pl.BlockSpec((seq, HEAD_DIM), lambda s, h, i: (s, Z_AV // HEAD_DIM + h)),
        pl.BlockSpec((1, r + 4, tq, tk), lambda s, h, i: (h, 0, 0, 0)),
        pl.BlockSpec((1, HEAD_DIM), lambda s, h, i: (0, 0)),
    ]
    kern = functools.partial(_attn_a_kernel, tq=tq, tk=tk, nk=seq // tk, r=r, post_scale=post_scale)
    return pl.pallas_call(
        kern,
        grid=(nseq, HEADS, nq),
        in_specs=in_specs,
        out_specs=pl.BlockSpec((tq, HEAD_DIM), lambda s, h, i: (s * nq + i, h)),
        out_shape=jax.ShapeDtypeStruct((nseq * seq, HEADS * HEAD_DIM), BF16),
        scratch_shapes=[pltpu.VMEM((2 * tq, tk), F32)] * 2,
        compiler_params=_cparams(("parallel", "parallel", "arbitrary")),
        name="attention_a",
    )(lam, z, z, z, bias, g)


def _merge_kernel(oa_ref, ob_ref, oc_ref, wa_ref, wb_ref, wc_ref, ga_ref, gb_ref, gc_ref, o_ref):
    def branch(o_r, w_r, g_r):
        y = jnp.dot(o_r[...], w_r[...], preferred_element_type=F32)
        return jax.nn.sigmoid(g_r[...].astype(F32)) * y

    merged = branch(oa_ref, wa_ref, ga_ref) + branch(ob_ref, wb_ref, gb_ref) + branch(oc_ref, wc_ref, gc_ref)
    o_ref[...] = merged.astype(o_ref.dtype)


def gated_merge(oa, ob, oc, wa, wb, wc, z, tm, tn):
    t, k = oa.shape
    n = wa.shape[1]
    o_spec = pl.BlockSpec((tm, k), lambda i, j: (i, 0))
    w_spec = pl.BlockSpec((k, tn), lambda i, j: (0, j))
    gate = lambda b: pl.BlockSpec((tm, tn), lambda i, j: (i, (Z_GATES + b * n) // tn + j))
    return pl.pallas_call(
        _merge_kernel,
        grid=(t // tm, n // tn),
        in_specs=[o_spec, o_spec, o_spec, w_spec, w_spec, w_spec, gate(0), gate(1), gate(2)],
        out_specs=pl.BlockSpec((tm, tn), lambda i, j: (i, j)),
        out_shape=jax.ShapeDtypeStruct((t, n), BF16),
        compiler_params=_cparams(("parallel", "arbitrary")),
        name="gated_merge",
    )(oa, ob, oc, wa, wb, wc, z, z, z)


def _mm_postnorm_kernel(a_ref, w_ref, g_ref, x_ref, o_ref):
    y = jnp.dot(a_ref[...], w_ref[...], preferred_element_type=F32)
    o_ref[...] = x_ref[...] + _rms(y, g_ref[...])


def mm_postnorm_residual(a, w, g, x, tm):
    t, k = a.shape
    n = w.shape[1]
    return pl.pallas_call(
        _mm_postnorm_kernel,
        grid=(t // tm,),
        in_specs=[
            pl.BlockSpec((tm, k), lambda i: (i, 0)),
            pl.BlockSpec((k, n), lambda i: (0, 0)),
            pl.BlockSpec((1, n), lambda i: (0, 0)),
            pl.BlockSpec((tm, n), lambda i: (i, 0)),
        ],
        out_specs=pl.BlockSpec((tm, n), lambda i: (i, 0)),
        out_shape=jax.ShapeDtypeStruct((t, n), F32),
        compiler_params=_cparams(("parallel",)),
        name="mm_postnorm_residual",
    )(a, w, g, x)


def _xattn_kernel(x_ref, gpre_ref, wq_ref, kv_ref, wo_ref, gpost_ref, o_ref, *, q_scale):
    x = x_ref[...]
    h = _rms(x, gpre_ref[...]).astype(BF16)
    q = (jnp.dot(h, wq_ref[...], preferred_element_type=F32) * q_scale).astype(BF16)
    kv_cols = X_HEADS * HEAD_DIM
    outs = []
    for hd in range(X_HEADS):
        sl = slice(hd * HEAD_DIM, (hd + 1) * HEAD_DIM)
        kh = kv_ref[0, :, sl]
        vh = kv_ref[0, :, kv_cols + hd * HEAD_DIM:kv_cols + (hd + 1) * HEAD_DIM]
        s = lax.dot_general(q[:, sl], kh, (((1,), (1,)), ((), ())), preferred_element_type=F32)
        p = jnp.exp2(s - jnp.max(s, axis=-1, keepdims=True))
        l = jnp.sum(p, axis=-1, keepdims=True)
        outs.append((jnp.dot(p.astype(BF16), vh, preferred_element_type=F32) / l).astype(BF16))
    o = jnp.concatenate(outs, axis=1)
    y = jnp.dot(o, wo_ref[...], preferred_element_type=F32)
    o_ref[...] = x + _rms(y, gpost_ref[...])


def memory_xattn(x, gpre, wq, memkv, mem_block, wo, gpost, tm):
    t, d = x.shape
    full = lambda a: pl.BlockSpec(a.shape, lambda i: (0,) * a.ndim)
    kern = functools.partial(_xattn_kernel, q_scale=HEAD_DIM ** -0.5 * LOG2E)
    return pl.pallas_call(
        kern,
        grid=(t // tm,),
        in_specs=[
            pl.BlockSpec((tm, d), lambda i: (i, 0)),
            full(gpre), full(wq),
            pl.BlockSpec((1,) + memkv.shape[1:], lambda i: (mem_block(i), 0, 0)),
            full(wo), full(gpost),
        ],
        out_specs=pl.BlockSpec((tm, d), lambda i: (i, 0)),
        out_shape=jax.ShapeDtypeStruct((t, d), F32),
        compiler_params=_cparams(("parallel",)),
        name="memory_xattn",
    )(x, gpre, wq, memkv, wo, gpost)


def _ffn_kernel(x_ref, gpre_ref, wg_ref, wu_ref, wd_ref, gpost_ref, o_ref, h_ref, acc_ref):
    f = pl.program_id(1)

    @pl.when(f == 0)
    def _():
        h_ref[...] = _rms(x_ref[...], gpre_ref[...]).astype(BF16)
        acc_ref[...] = jnp.zeros_like(acc_ref)

    h = h_ref[...]
    a = jnp.dot(h, wg_ref[...], preferred_element_type=F32)
    b = jnp.dot(h, wu_ref[...], preferred_element_type=F32)
    t = (a * jax.nn.sigmoid(a) * b).astype(BF16)
    acc_ref[...] += jnp.dot(t, wd_ref[...], preferred_element_type=F32)

    @pl.when(f == pl.num_programs(1) - 1)
    def _():
        o_ref[...] = x_ref[...] + _rms(acc_ref[...], gpost_ref[...])


def swiglu_ffn(x, gpre, wg, wu, wd, gpost, tm, tf):
    t, d = x.shape
    ff = wg.shape[1]
    return pl.pallas_call(
        _ffn_kernel,
        grid=(t // tm, ff // tf),
        in_specs=[
            pl.BlockSpec((tm, d), lambda i, f: (i, 0)),
            pl.BlockSpec((1, d), lambda i, f: (0, 0)),
            pl.BlockSpec((d, tf), lambda i, f: (0, f)),
            pl.BlockSpec((d, tf), lambda i, f: (0, f)),
            pl.BlockSpec((tf, d), lambda i, f: (f, 0)),
            pl.BlockSpec((1, d), lambda i, f: (0, 0)),
        ],
        out_specs=pl.BlockSpec((tm, d), lambda i, f: (i, 0)),
        out_shape=jax.ShapeDtypeStruct((t, d), F32),
        scratch_shapes=[pltpu.VMEM((tm, d), BF16), pltpu.VMEM((tm, d), F32)],
        compiler_params=_cparams(("parallel", "arbitrary")),
        name="swiglu_ffn",
    )(x, gpre, wg, wu, wd, gpost)


def _rotary_tables(seq):
    pos = jnp.arange(seq)

    def cs(p):
        inv = ROPE_THETA ** (-jnp.arange(0, 64, 2, dtype=F32) / 64)
        ang = p.astype(F32)[:, None] * inv[None, :]
        c, s = jnp.cos(ang), jnp.sin(ang)
        return jnp.concatenate([c, c], axis=-1), jnp.concatenate([-s, s], axis=-1)

    c_row, s_row = cs(pos // GRID_W)
    c_col, s_col = cs(pos % GRID_W)
    c_tok, s_tok = cs(pos)
    return (jnp.concatenate([c_row, c_col], axis=-1), jnp.concatenate([s_row, s_col], axis=-1),
            jnp.concatenate([c_tok, c_tok], axis=-1), jnp.concatenate([s_tok, s_tok], axis=-1))


def _t5_bucket(rel):
    nb = NUM_BUCKETS // 2
    max_exact = nb // 2
    ret = (rel > 0).astype(jnp.int32) * nb
    n = jnp.abs(rel)
    nf = jnp.maximum(n, 1).astype(F32)
    large = max_exact + (jnp.log(nf / max_exact) / math.log(MAX_DISTANCE / max_exact)
                         * (nb - max_exact)).astype(jnp.int32)
    large = jnp.minimum(large, nb - 1)
    return ret + jnp.where(n < max_exact, n, large)


def _bias_kernel(tab_ref, o_ref, *, tq, tk, r):
    h = pl.program_id(1)
    rel = ((pl.program_id(0) - (r + 1)) * tq + lax.broadcasted_iota(jnp.int32, (tq, tk), 1)
           - lax.broadcasted_iota(jnp.int32, (tq, tk), 0))
    bucket = _t5_bucket(rel)
    v = jnp.full((tq, tk), tab_ref[h, 0], F32)
    for b in range(1, NUM_BUCKETS):
        v = jnp.where(bucket == b, tab_ref[h, b], v)
    o_ref[0, 0] = v * LOG2E


def _bias_tiles(rel_bias, tq, tk):
    assert tq >= 128 and tk % tq == 0
    r = tk // tq
    return pl.pallas_call(
        functools.partial(_bias_kernel, tq=tq, tk=tk, r=r),
        grid=(r + 4, HEADS),
        in_specs=[pl.BlockSpec(memory_space=pltpu.SMEM)],
        out_specs=pl.BlockSpec((1, 1, tq, tk), lambda t, h: (h, t, 0, 0)),
        out_shape=jax.ShapeDtypeStruct((HEADS, r + 4, tq, tk), F32),
        compiler_params=_cparams(("parallel", "parallel")),
        name="bias_tiles",
    )(rel_bias.T.astype(F32))


def kernel(x_prompt, x_sample, mem_prompt, mem_sample, rel_bias, g_mix_pre, g_mix_post, w_in, lam_q1, lam_k1, lam_q2, lam_k2, g_a_out, g_b_q, g_b_k, g_c_q, g_c_kv, w_c_q_up, w_c_kv_up, w_br_a, w_br_b, w_br_c, w_mix_out, g_x_pre, g_x_post, g_mem, w_x_q, w_x_kv, w_x_out, g_ffn_pre, g_ffn_post, w_ffn_gate, w_ffn_up, w_ffn_down):
    depth = w_in.shape[0]
    d = x_prompt.shape[-1]
    tm_row = 512
    tq_a, tk_a = 256, 1024
    tq, tk = 512, 1024

    groups = [(x.reshape(-1, d), m, x.shape[0], x.shape[1])
              for x, m in ((x_prompt, mem_prompt), (x_sample, mem_sample))]
    tables = _rotary_tables(max(g[3] for g in groups))
    bias = _bias_tiles(rel_bias, tq_a, tk_a)
    row = lambda g: g.reshape(1, -1).astype(F32)
    in_scale = jnp.concatenate([jnp.full((1, Z_AK), A_QK_DIM ** -0.5 * LOG2E, F32),
                                jnp.ones((1, Z_COLS - Z_AK), F32)], axis=-1)
    ones_kv = jnp.ones((1, w_x_kv.shape[-1]), F32)
    kv_group = HEADS // B_KV_HEADS

    xs = [g[0] for g in groups]
    for l in range(depth):
        lam_init = 0.8 - 0.6 * math.exp(-0.3 * l)
        lam = (jnp.exp(jnp.sum(lam_q1[l].astype(F32) * lam_k1[l].astype(F32)))
               - jnp.exp(jnp.sum(lam_q2[l].astype(F32) * lam_k2[l].astype(F32))) + lam_init).reshape(1)

        w_in_l = jnp.concatenate(
            [w_in[l, :, :Z_RAW_SPLIT].astype(BF16), jnp.zeros((d, Z_GATES - Z_RAW_SPLIT), BF16),
             w_in[l, :, Z_RAW_SPLIT:].astype(BF16)], axis=-1)
        wq3 = w_c_q_up[l].reshape(C_Q_RANK, HEADS, C_NOPE + C_ROPE)
        w_cq = jnp.concatenate(
            [wq3[..., :C_NOPE].reshape(C_Q_RANK, HEADS * C_NOPE),
             jnp.pad(wq3[..., C_NOPE:], ((0, 0), (0, 0), (0, HEAD_DIM - C_ROPE))).reshape(
                 C_Q_RANK, HEADS * HEAD_DIM)], axis=-1).astype(BF16)
        w_ckv = w_c_kv_up[l].astype(BF16)
        w_a, w_b, w_c = w_br_a[l].astype(BF16), w_br_b[l].astype(BF16), w_br_c[l].astype(BF16)
        w_mix = w_mix_out[l].astype(BF16)
        w_xq, w_xkv, w_xo = w_x_q[l].astype(BF16), w_x_kv[l].astype(BF16), w_x_out[l].astype(BF16)
        w_fg, w_fu, w_fd = w_ffn_gate[l].astype(BF16), w_ffn_up[l].astype(BF16), w_ffn_down[l].astype(BF16)

        for gi, (_, mem, nseq, seq) in enumerate(groups):
            x = xs[gi]
            blocks_per_seq = seq // tm_row
            z = rms_matmul(x, row(g_mix_pre[l]), w_in_l, in_scale, tm=1024, tn=512)
            qb, kb, cq, ckv, kr = mixer_prep(z, tables, lambda i: i % blocks_per_seq, row(g_b_q[l]), row(g_b_k[l]),
                                             row(g_c_q[l]), row(g_c_kv[l]), w_cq, w_ckv, tm_row)
            oa = attention_a(lam, z, bias, row(g_a_out[l]), nseq, seq, tq_a, tk_a, 1.0 - lam_init)
            ob = attention([(qb, lambda h: h)],
                           [(kb, lambda h: h // kv_group)],
                           (z, lambda h: Z_BV // HEAD_DIM + h // kv_group),
                           nseq, seq, tq, tk)
            oc = attention([(cq, lambda h: h), (cq, lambda h: HEADS + h)],
                           [(ckv, lambda h: 2 * h), (kr, lambda h: 0)],
                           (ckv, lambda h: 2 * h + 1),
                           nseq, seq, tq, tk)
            merged = gated_merge(oa, ob, oc, w_a, w_b, w_c, z, tm=1024, tn=512)
            x = mm_postnorm_residual(merged, w_mix, row(g_mix_post[l]), x, tm=tm_row)

            mem2 = mem.reshape(nseq * MEM_TOKENS, d)
            memkv = rms_matmul(mem2, row(g_mem[l]), w_xkv, ones_kv, tm=mem2.shape[0], tn=512)
            memkv = memkv.reshape(nseq, MEM_TOKENS, -1)
            x = memory_xattn(x, row(g_x_pre[l]), w_xq, memkv, lambda i: i // blocks_per_seq, w_xo,
                             row(g_x_post[l]), tm_row)
            xs[gi] = swiglu_ffn(x, row(g_ffn_pre[l]), w_fg, w_fu, w_fd, row(g_ffn_post[l]), tm=tm_row, tf=512)

    return tuple(x.reshape(g[2], g[3], d) for x, g in zip(xs, groups))
```

```python
import functools
import math

import jax
import jax.numpy as jnp
from jax import lax
from jax.experimental import pallas as pl
from jax.experimental.pallas import tpu as pltpu

F32 = jnp.float32
BF16 = jnp.bfloat16

EPS = 1e-6
ROPE_THETA = 10000.0
LOG2E = 1.4426950408889634
GRID_W = 64
NUM_BUCKETS = 32
MAX_DISTANCE = 128

D_MODEL = 2048
HEADS = 8
HEAD_DIM = 128
A_QK_DIM = 64
B_KV_HEADS = 2
C_Q_RANK = 512
C_KV_RANK = 256
C_NOPE = 128
C_ROPE = 64
X_HEADS = 4
MEM_TOKENS = 256

Z_AQ, Z_AK, Z_AV, Z_BQ, Z_BK, Z_BV, Z_CQA, Z_CKVA, Z_CKR = 0, 1024, 2048, 3072, 4096, 4352, 4608, 5120, 5376
Z_RAW_SPLIT = 5440
Z_GATES = 5632
Z_COLS = Z_GATES + 3 * D_MODEL

VMEM_LIMIT = 56 * 1024 * 1024


def _cparams(sem):
    return pltpu.CompilerParams(dimension_semantics=sem, vmem_limit_bytes=VMEM_LIMIT)


def _rms(x, g):
    return x * lax.rsqrt(jnp.mean(x * x, axis=-1, keepdims=True) + EPS) * g


def _swap32(x):
    lane = lax.broadcasted_iota(jnp.int32, x.shape, 1)
    return jnp.where((lane % 64) < 32, pltpu.roll(x, 96, 1), pltpu.roll(x, 32, 1))


def _rope(x, cos, sin_signed):
    return x * cos + _swap32(x) * sin_signed


def _rms_matmul_kernel(x_ref, g_ref, w_ref, cs_ref, o_ref, xn_ref):
    @pl.when(pl.program_id(1) == 0)
    def _():
        xn_ref[...] = _rms(x_ref[...].astype(F32), g_ref[...]).astype(BF16)

    acc = jnp.dot(xn_ref[...], w_ref[...], preferred_element_type=F32)
    o_ref[...] = (acc * cs_ref[...]).astype(o_ref.dtype)


def rms_matmul(x, g, w, colscale, tm, tn, out_dtype=BF16):
    m, k = x.shape
    n = w.shape[1]
    return pl.pallas_call(
        _rms_matmul_kernel,
        grid=(m // tm, n // tn),
        in_specs=[
            pl.BlockSpec((tm, k), lambda i, j: (i, 0)),
            pl.BlockSpec((1, k), lambda i, j: (0, 0)),
            pl.BlockSpec((k, tn), lambda i, j: (0, j)),
            pl.BlockSpec((1, tn), lambda i, j: (0, j)),
        ],
        out_specs=pl.BlockSpec((tm, tn), lambda i, j: (i, j)),
        out_shape=jax.ShapeDtypeStruct((m, n), out_dtype),
        scratch_shapes=[pltpu.VMEM((tm, k), BF16)],
        compiler_params=_cparams(("parallel", "arbitrary")),
        name="rms_matmul",
    )(x, g, w, colscale)


def _prep_kernel(bq_ref, bkv_ref, cqa_ref, ckk_ref, cos_a_ref, sin_a_ref, cos_r_ref, sin_r_ref,
                 gbq_ref, gbk_ref, gcq_ref, gckv_ref, wq_ref, wkv_ref,
                 qb_out, kb_out, cq_out, ckv_out, kr_out, *, b_scale, c_scale):
    cos_a, sin_a = cos_a_ref[...], sin_a_ref[...]
    cos_r, sin_r = cos_r_ref[...], sin_r_ref[...]

    for h in range(HEADS):
        sl = slice(h * HEAD_DIM, (h + 1) * HEAD_DIM)
        y = _rms(bq_ref[:, sl].astype(F32), gbq_ref[...])
        qb_out[:, sl] = (_rope(y, cos_a, sin_a) * b_scale).astype(BF16)
    for n in range(B_KV_HEADS):
        sl = slice(n * HEAD_DIM, (n + 1) * HEAD_DIM)
        y = _rms(bkv_ref[:, sl].astype(F32), gbk_ref[...])
        kb_out[:, sl] = _rope(y, cos_a, sin_a).astype(BF16)

    xq = _rms(cqa_ref[...].astype(F32), gcq_ref[...]).astype(BF16)
    cq = jnp.dot(xq, wq_ref[...], preferred_element_type=F32)
    nope_cols = HEADS * C_NOPE
    cq_out[:, :nope_cols] = (cq[:, :nope_cols] * c_scale).astype(BF16)
    for h in range(HEADS):
        sl = slice(nope_cols + h * HEAD_DIM, nope_cols + (h + 1) * HEAD_DIM)
        cq_out[:, sl] = (_rope(cq[:, sl], cos_r, sin_r) * c_scale).astype(BF16)

    xkv = _rms(ckk_ref[:, :C_KV_RANK].astype(F32), gckv_ref[...]).astype(BF16)
    ckv_out[...] = jnp.dot(xkv, wkv_ref[...], preferred_element_type=F32).astype(BF16)
    kr = ckk_ref[:, C_KV_RANK:C_KV_RANK + HEAD_DIM].astype(F32)
    kr_out[...] = _rope(kr, cos_r, sin_r).astype(BF16)


def mixer_prep(z, tables, pos_block, gbq, gbk, gcq, gckv, wq, wkv, tm):
    t = z.shape[0]
    cos_a, sin_a, cos_r, sin_r = tables
    row = lambda w: pl.BlockSpec((tm, w), lambda i: (i, 0))
    tab = pl.BlockSpec((tm, HEAD_DIM), lambda i: (pos_block(i), 0))
    full = lambda a: pl.BlockSpec(a.shape, lambda i: (0,) * a.ndim)
    kern = functools.partial(_prep_kernel, b_scale=HEAD_DIM ** -0.5 * LOG2E,
                             c_scale=(C_NOPE + C_ROPE) ** -0.5 * LOG2E)
    return pl.pallas_call(
        kern,
        grid=(t // tm,),
        in_specs=[
            pl.BlockSpec((tm, 1024), lambda i: (i, Z_BQ // 1024)),
            pl.BlockSpec((tm, 512), lambda i: (i, Z_BK // 512)),
            pl.BlockSpec((tm, 512), lambda i: (i, Z_CQA // 512)),
            pl.BlockSpec((tm, 512), lambda i: (i, Z_CKVA // 512)),
            tab, tab, tab, tab,
            full(gbq), full(gbk), full(gcq), full(gckv), full(wq), full(wkv),
        ],
        out_specs=[row(1024), row(256), row(2048), row(2048), row(HEAD_DIM)],
        out_shape=[
            jax.ShapeDtypeStruct((t, 1024), BF16),
            jax.ShapeDtypeStruct((t, 256), BF16),
            jax.ShapeDtypeStruct((t, 2048), BF16),
            jax.ShapeDtypeStruct((t, 2048), BF16),
            jax.ShapeDtypeStruct((t, HEAD_DIM), BF16),
        ],
        compiler_params=_cparams(("parallel",)),
        name="mixer_prep",
    )(z, z, z, z, cos_a, sin_a, cos_r, sin_r, gbq, gbk, gcq, gckv, wq, wkv)


def _flash(q, k_refs, v_ref, s_refs, nk, tk, bias_fn=None):
    m_rows = q.shape[0]

    def scores(i):
        ks = [r[i * tk:(i + 1) * tk, :] for r in k_refs]
        k = ks[0] if len(ks) == 1 else jnp.concatenate(ks, axis=1)
        return lax.dot_general(q, k, (((1,), (1,)), ((), ())), preferred_element_type=F32)

    dv = v_ref.shape[1]
    ones = jnp.ones((tk, dv), BF16)

    def update(i, s, carry):
        m, acc = carry
        if bias_fn is not None:
            s = bias_fn(i, s)
        m_new = jnp.maximum(m, jnp.max(s, axis=-1, keepdims=True))
        p = jnp.exp2((s - m_new).astype(BF16))
        alpha = jnp.exp2(m - m_new)
        v_aug = jnp.concatenate([v_ref[i * tk:(i + 1) * tk, :], ones], axis=1)
        pv = jnp.dot(p, v_aug, preferred_element_type=F32)
        return m_new, alpha * acc + pv

    carry = (jnp.full((m_rows, 1), -jnp.inf, F32), jnp.zeros((m_rows, 2 * dv), F32))
    s_refs[0][...] = scores(0)
    for i in range(nk):
        if i + 1 < nk:
            s_refs[(i + 1) % 2][...] = scores(i + 1)
        carry = update(i, s_refs[i % 2][...], carry)
    m, acc = carry
    return m, acc[:, dv:], acc[:, :dv]


def _attn_kernel(*refs, n_q, n_k, nk, tk, streams):
    n_s = 2 * streams
    q_refs, k_refs = refs[:n_q], refs[n_q:n_q + n_k]
    v_ref, o_ref, s_refs = refs[n_q + n_k], refs[-n_s - 1], refs[-n_s:]
    rows = o_ref.shape[0] // streams
    for t in range(streams):
        sl = slice(t * rows, (t + 1) * rows)
        qs = [r[sl, :] for r in q_refs]
        q = qs[0] if n_q == 1 else jnp.concatenate(qs, axis=1)
        _, l, acc = _flash(q, k_refs, v_ref, s_refs[2 * t:2 * t + 2], nk, tk)
        o_ref[sl, :] = (acc / l).astype(o_ref.dtype)


def _attn_a_kernel(lam_ref, q_ref, k_ref, v_ref, bias_ref, g_ref, *rest, tq, tk, nk, r, post_scale):
    o_ref, s_refs = rest[-3], rest[-2:]
    qi = pl.program_id(2)
    q = q_ref[...]
    lane = lax.broadcasted_iota(jnp.int32, q.shape, 1)
    zero = jnp.zeros_like(q)
    q2 = jnp.concatenate([jnp.where(lane < A_QK_DIM, q, zero), jnp.where(lane >= A_QK_DIM, q, zero)], axis=0)

    def bias_fn(i, s):
        idx = jnp.clip(i * r - qi, -r - 1, 2) + r + 1
        b = bias_ref[0, idx]
        return jnp.concatenate([s[:tq] + b, s[tq:] + b], axis=0)

    _, l, acc = _flash(q2, [k_ref], v_ref, s_refs, nk, tk, bias_fn)
    o = acc / l
    d = o[:tq] - lam_ref[0] * o[tq:]
    o_ref[...] = (_rms(d, g_ref[...]) * post_scale).astype(o_ref.dtype)


def attention(q_parts, k_parts, v_part, nseq, seq, tq, tk, streams):
    rows = tq * streams
    nq = seq // rows
    in_specs, args = [], []
    for a, cf in q_parts:
        in_specs.append(pl.BlockSpec((rows, HEAD_DIM), lambda s, h, i, cf=cf: (s * nq + i, cf(h))))
        args.append(a)
    for a, cf in list(k_parts) + [v_part]:
        in_specs.append(pl.BlockSpec((seq, HEAD_DIM), lambda s, h, i, cf=cf: (s, cf(h))))
        args.append(a)
    kern = functools.partial(_attn_kernel, n_q=len(q_parts), n_k=len(k_parts), nk=seq // tk, tk=tk,
                             streams=streams)
    return pl.pallas_call(
        kern,
        grid=(nseq, HEADS, nq),
        in_specs=in_specs,
        out_specs=pl.BlockSpec((rows, HEAD_DIM), lambda s, h, i: (s * nq + i, h)),
        out_shape=jax.ShapeDtypeStruct((nseq * seq, HEADS * HEAD_DIM), BF16),
        scratch_shapes=[pltpu.VMEM((tq, tk), F32)] * (2 * streams),
        compiler_params=_cparams(("parallel", "parallel", "arbitrary")),
        name="attention",
    )(*args)


def attention_a(lam, z, bias, g, nseq, seq, tq, tk, post_scale):
    nq = seq // tq
    r = tk // tq
    in_specs = [
        pl.BlockSpec(memory_space=pltpu.SMEM),
        pl.BlockSpec((tq, HEAD_DIM), lambda s, h, i: (s * nq + i, Z_AQ // HEAD_DIM + h)),
        pl.BlockSpec((seq, HEAD_DIM), lambda s, h, i: (s, Z_AK // HEAD_DIM + h)),
        pl.BlockSpec((seq, HEAD_DIM), lambda s, h, i: (s, Z_AV // HEAD_DIM + h)),
        pl.BlockSpec((1, r + 4, tq, tk), lambda s, h, i: (h, 0, 0, 0)),
        pl.BlockSpec((1, HEAD_DIM), lambda s, h, i: (0, 0)),
    ]
    kern = functools.partial(_attn_a_kernel, tq=tq, tk=tk, nk=seq // tk, r=r, post_scale=post_scale)
    return pl.pallas_call(
        kern,
        grid=(nseq, HEADS, nq),
        in_specs=in_specs,
        out_specs=pl.BlockSpec((tq, HEAD_DIM), lambda s, h, i: (s * nq + i, h)),
        out_shape=jax.ShapeDtypeStruct((nseq * seq, HEADS * HEAD_DIM), BF16),
        scratch_shapes=[pltpu.VMEM((2 * tq, tk), F32)] * 2,
        compiler_params=_cparams(("parallel", "parallel", "arbitrary")),
        name="attention_a",
    )(lam, z, z, z, bias, g)


def _merge_kernel(oa_ref, ob_ref, oc_ref, wa_ref, wb_ref, wc_ref, ga_ref, gb_ref, gc_ref, o_ref):
    def branch(o_r, w_r, g_r):
        y = jnp.dot(o_r[...], w_r[...], preferred_element_type=F32)
        return jax.nn.sigmoid(g_r[...].astype(F32)) * y

    merged = branch(oa_ref, wa_ref, ga_ref) + branch(ob_ref, wb_ref, gb_ref) + branch(oc_ref, wc_ref, gc_ref)
    o_ref[...] = merged.astype(o_ref.dtype)


def gated_merge(oa, ob, oc, wa, wb, wc, z, tm, tn):
    t, k = oa.shape
    n = wa.shape[1]
    o_spec = pl.BlockSpec((tm, k), lambda i, j: (i, 0))
    w_spec = pl.BlockSpec((k, tn), lambda i, j: (0, j))
    gate = lambda b: pl.BlockSpec((tm, tn), lambda i, j: (i, (Z_GATES + b * n) // tn + j))
    return pl.pallas_call(
        _merge_kernel,
        grid=(t // tm, n // tn),
        in_specs=[o_spec, o_spec, o_spec, w_spec, w_spec, w_spec, gate(0), gate(1), gate(2)],
        out_specs=pl.BlockSpec((tm, tn), lambda i, j: (i, j)),
        out_shape=jax.ShapeDtypeStruct((t, n), BF16),
        compiler_params=_cparams(("parallel", "arbitrary")),
        name="gated_merge",
    )(oa, ob, oc, wa, wb, wc, z, z, z)


def _mm_postnorm_kernel(a_ref, w_ref, g_ref, x_ref, o_ref):
    y = jnp.dot(a_ref[...], w_ref[...], preferred_element_type=F32)
    o_ref[...] = x_ref[...] + _rms(y, g_ref[...])


def mm_postnorm_residual(a, w, g, x, tm):
    t, k = a.shape
    n = w.shape[1]
    return pl.pallas_call(
        _mm_postnorm_kernel,
        grid=(t // tm,),
        in_specs=[
            pl.BlockSpec((tm, k), lambda i: (i, 0)),
            pl.BlockSpec((k, n), lambda i: (0, 0)),
            pl.BlockSpec((1, n), lambda i: (0, 0)),
            pl.BlockSpec((tm, n), lambda i: (i, 0)),
        ],
        out_specs=pl.BlockSpec((tm, n), lambda i: (i, 0)),
        out_shape=jax.ShapeDtypeStruct((t, n), F32),
        compiler_params=_cparams(("parallel",)),
        name="mm_postnorm_residual",
    )(a, w, g, x)


def _xattn_kernel(x_ref, gpre_ref, wq_ref, kv_ref, wo_ref, gpost_ref, o_ref, *, q_scale):
    x = x_ref[...]
    h = _rms(x, gpre_ref[...]).astype(BF16)
    q = (jnp.dot(h, wq_ref[...], preferred_element_type=F32) * q_scale).astype(BF16)
    kv_cols = X_HEADS * HEAD_DIM
    outs = []
    for hd in range(X_HEADS):
        sl = slice(hd * HEAD_DIM, (hd + 1) * HEAD_DIM)
        kh = kv_ref[0, :, sl]
        vh = kv_ref[0, :, kv_cols + hd * HEAD_DIM:kv_cols + (hd + 1) * HEAD_DIM]
        s = lax.dot_general(q[:, sl], kh, (((1,), (1,)), ((), ())), preferred_element_type=F32)
        p = jnp.exp2(s - jnp.max(s, axis=-1, keepdims=True))
        l = jnp.sum(p, axis=-1, keepdims=True)
        outs.append((jnp.dot(p.astype(BF16), vh, preferred_element_type=F32) / l).astype(BF16))
    o = jnp.concatenate(outs, axis=1)
    y = jnp.dot(o, wo_ref[...], preferred_element_type=F32)
    o_ref[...] = x + _rms(y, gpost_ref[...])


def memory_xattn(x, gpre, wq, memkv, mem_block, wo, gpost, tm):
    t, d = x.shape
    full = lambda a: pl.BlockSpec(a.shape, lambda i: (0,) * a.ndim)
    kern = functools.partial(_xattn_kernel, q_scale=HEAD_DIM ** -0.5 * LOG2E)
    return pl.pallas_call(
        kern,
        grid=(t // tm,),
        in_specs=[
            pl.BlockSpec((tm, d), lambda i: (i, 0)),
            full(gpre), full(wq),
            pl.BlockSpec((1,) + memkv.shape[1:], lambda i: (mem_block(i), 0, 0)),
            full(wo), full(gpost),
        ],
        out_specs=pl.BlockSpec((tm, d), lambda i: (i, 0)),
        out_shape=jax.ShapeDtypeStruct((t, d), F32),
        compiler_params=_cparams(("parallel",)),
        name="memory_xattn",
    )(x, gpre, wq, memkv, wo, gpost)


def _ffn_kernel(x_ref, gpre_ref, wg_ref, wu_ref, wd_ref, gpost_ref, o_ref, h_ref, acc_ref):
    f = pl.program_id(1)

    @pl.when(f == 0)
    def _():
        h_ref[...] = _rms(x_ref[...], gpre_ref[...]).astype(BF16)
        acc_ref[...] = jnp.zeros_like(acc_ref)

    h = h_ref[...]
    a = jnp.dot(h, wg_ref[...], preferred_element_type=F32)
    b = jnp.dot(h, wu_ref[...], preferred_element_type=F32)
    t = (a * jax.nn.sigmoid(a) * b).astype(BF16)
    acc_ref[...] += jnp.dot(t, wd_ref[...], preferred_element_type=F32)

    @pl.when(f == pl.num_programs(1) - 1)
    def _():
        o_ref[...] = x_ref[...] + _rms(acc_ref[...], gpost_ref[...])


def swiglu_ffn(x, gpre, wg, wu, wd, gpost, tm, tf):
    t, d = x.shape
    ff = wg.shape[1]
    return pl.pallas_call(
        _ffn_kernel,
        grid=(t // tm, ff // tf),
        in_specs=[
            pl.BlockSpec((tm, d), lambda i, f: (i, 0)),
            pl.BlockSpec((1, d), lambda i, f: (0, 0)),
            pl.BlockSpec((d, tf), lambda i, f: (0, f)),
            pl.BlockSpec((d, tf), lambda i, f: (0, f)),
            pl.BlockSpec((tf, d), lambda i, f: (f, 0)),
            pl.BlockSpec((1, d), lambda i, f: (0, 0)),
        ],
        out_specs=pl.BlockSpec((tm, d), lambda i, f: (i, 0)),
        out_shape=jax.ShapeDtypeStruct((t, d), F32),
        scratch_shapes=[pltpu.VMEM((tm, d), BF16), pltpu.VMEM((tm, d), F32)],
        compiler_params=_cparams(("parallel", "arbitrary")),
        name="swiglu_ffn",
    )(x, gpre, wg, wu, wd, gpost)


def _rotary_tables(seq):
    pos = jnp.arange(seq)

    def cs(p):
        inv = ROPE_THETA ** (-jnp.arange(0, 64, 2, dtype=F32) / 64)
        ang = p.astype(F32)[:, None] * inv[None, :]
        c, s = jnp.cos(ang), jnp.sin(ang)
        return jnp.concatenate([c, c], axis=-1), jnp.concatenate([-s, s], axis=-1)

    c_row, s_row = cs(pos // GRID_W)
    c_col, s_col = cs(pos % GRID_W)
    c_tok, s_tok = cs(pos)
    return (jnp.concatenate([c_row, c_col], axis=-1), jnp.concatenate([s_row, s_col], axis=-1),
            jnp.concatenate([c_tok, c_tok], axis=-1), jnp.concatenate([s_tok, s_tok], axis=-1))


def _t5_bucket(rel):
    nb = NUM_BUCKETS // 2
    max_exact = nb // 2
    ret = (rel > 0).astype(jnp.int32) * nb
    n = jnp.abs(rel)
    nf = jnp.maximum(n, 1).astype(F32)
    large = max_exact + (jnp.log(nf / max_exact) / math.log(MAX_DISTANCE / max_exact)
                         * (nb - max_exact)).astype(jnp.int32)
    large = jnp.minimum(large, nb - 1)
    return ret + jnp.where(n < max_exact, n, large)


def _bias_kernel(tab_ref, o_ref, *, tq, tk, r):
    h = pl.program_id(1)
    rel = ((pl.program_id(0) - (r + 1)) * tq + lax.broadcasted_iota(jnp.int32, (tq, tk), 1)
           - lax.broadcasted_iota(jnp.int32, (tq, tk), 0))
    bucket = _t5_bucket(rel)
    v = jnp.full((tq, tk), tab_ref[h, 0], F32)
    for b in range(1, NUM_BUCKETS):
        v = jnp.where(bucket == b, tab_ref[h, b], v)
    o_ref[0, 0] = v * LOG2E


def _bias_tiles(rel_bias, tq, tk):
    assert tq >= 128 and tk % tq == 0
    r = tk // tq
    return pl.pallas_call(
        functools.partial(_bias_kernel, tq=tq, tk=tk, r=r),
        grid=(r + 4, HEADS),
        in_specs=[pl.BlockSpec(memory_space=pltpu.SMEM)],
        out_specs=pl.BlockSpec((1, 1, tq, tk), lambda t, h: (h, t, 0, 0)),
        out_shape=jax.ShapeDtypeStruct((HEADS, r + 4, tq, tk), F32),
        compiler_params=_cparams(("parallel", "parallel")),
        name="bias_tiles",
    )(rel_bias.T.astype(F32))


def kernel(x_prompt, x_sample, mem_prompt, mem_sample, rel_bias, g_mix_pre, g_mix_post, w_in, lam_q1, lam_k1, lam_q2, lam_k2, g_a_out, g_b_q, g_b_k, g_c_q, g_c_kv, w_c_q_up, w_c_kv_up, w_br_a, w_br_b, w_br_c, w_mix_out, g_x_pre, g_x_post, g_mem, w_x_q, w_x_kv, w_x_out, g_ffn_pre, g_ffn_post, w_ffn_gate, w_ffn_up, w_ffn_down):
    depth = w_in.shape[0]
    d = x_prompt.shape[-1]
    tm_row = 512
    tq_a, tk_a = 256, 1024
    tq, tk, streams = 512, 1024, 1

    groups = [(x.reshape(-1, d), m, x.shape[0], x.shape[1])
              for x, m in ((x_prompt, mem_prompt), (x_sample, mem_sample))]
    tables = _rotary_tables(max(g[3] for g in groups))
    bias = _bias_tiles(rel_bias, tq_a, tk_a)
    row = lambda g: g.reshape(1, -1).astype(F32)
    in_scale = jnp.concatenate([jnp.full((1, Z_AK), A_QK_DIM ** -0.5 * LOG2E, F32),
                                jnp.ones((1, Z_COLS - Z_AK), F32)], axis=-1)
    ones_kv = jnp.ones((1, w_x_kv.shape[-1]), F32)
    kv_group = HEADS // B_KV_HEADS

    xs = [g[0] for g in groups]
    for l in range(depth):
        lam_init = 0.8 - 0.6 * math.exp(-0.3 * l)
        lam = (jnp.exp(jnp.sum(lam_q1[l].astype(F32) * lam_k1[l].astype(F32)))
               - jnp.exp(jnp.sum(lam_q2[l].astype(F32) * lam_k2[l].astype(F32))) + lam_init).reshape(1)

        w_in_l = jnp.concatenate(
            [w_in[l, :, :Z_RAW_SPLIT].astype(BF16), jnp.zeros((d, Z_GATES - Z_RAW_SPLIT), BF16),
             w_in[l, :, Z_RAW_SPLIT:].astype(BF16)], axis=-1)
        wq3 = w_c_q_up[l].reshape(C_Q_RANK, HEADS, C_NOPE + C_ROPE)
        w_cq = jnp.concatenate(
            [wq3[..., :C_NOPE].reshape(C_Q_RANK, HEADS * C_NOPE),
             jnp.pad(wq3[..., C_NOPE:], ((0, 0), (0, 0), (0, HEAD_DIM - C_ROPE))).reshape(
                 C_Q_RANK, HEADS * HEAD_DIM)], axis=-1).astype(BF16)
        w_ckv = w_c_kv_up[l].astype(BF16)
        w_a, w_b, w_c = w_br_a[l].astype(BF16), w_br_b[l].astype(BF16), w_br_c[l].astype(BF16)
        w_mix = w_mix_out[l].astype(BF16)
        w_xq, w_xkv, w_xo = w_x_q[l].astype(BF16), w_x_kv[l].astype(BF16), w_x_out[l].astype(BF16)
        w_fg, w_fu, w_fd = w_ffn_gate[l].astype(BF16), w_ffn_up[l].astype(BF16), w_ffn_down[l].astype(BF16)

        for gi, (_, mem, nseq, seq) in enumerate(groups):
            x = xs[gi]
            blocks_per_seq = seq // tm_row
            z = rms_matmul(x, row(g_mix_pre[l]), w_in_l, in_scale, tm=1024, tn=512)
            qb, kb, cq, ckv, kr = mixer_prep(z, tables, lambda i: i % blocks_per_seq, row(g_b_q[l]), row(g_b_k[l]),
                                             row(g_c_q[l]), row(g_c_kv[l]), w_cq, w_ckv, tm_row)
            oa = attention_a(lam, z, bias, row(g_a_out[l]), nseq, seq, tq_a, tk_a, 1.0 - lam_init)
            ob = attention([(qb, lambda h: h)],
                           [(kb, lambda h: h // kv_group)],
                           (z, lambda h: Z_BV // HEAD_DIM + h // kv_group),
                           nseq, seq, tq, tk, streams)
            oc = attention([(cq, lambda h: h), (cq, lambda h: HEADS + h)],
                           [(ckv, lambda h: 2 * h), (kr, lambda h: 0)],
                           (ckv, lambda h: 2 * h + 1),
                           nseq, seq, tq, tk, streams)
            merged = gated_merge(oa, ob, oc, w_a, w_b, w_c, z, tm=1024, tn=512)
            x = mm_postnorm_residual(merged, w_mix, row(g_mix_post[l]), x, tm=tm_row)

            mem2 = mem.reshape(nseq * MEM_TOKENS, d)
            memkv = rms_matmul(mem2, row(g_mem[l]), w_xkv, ones_kv, tm=mem2.shape[0], tn=512)
            memkv = memkv.reshape(nseq, MEM_TOKENS, -1)
            x = memory_xattn(x, row(g_x_pre[l]), w_xq, memkv, lambda i: i // blocks_per_seq, w_xo,
                             row(g_x_post[l]), tm_row)
            xs[gi] = swiglu_ffn(x, row(g_ffn_pre[l]), w_fg, w_fu, w_fd, row(g_ffn_post[l]), tm=tm_row, tf=512)

    return tuple(x.reshape(g[2], g[3], d) for x, g in zip(xs, groups))
```

```python
import functools
import math

import jax
import jax.numpy as jnp
from jax import lax
from jax.experimental import pallas as pl
from jax.experimental.pallas import tpu as pltpu

F32 = jnp.float32
BF16 = jnp.bfloat16

EPS = 1e-6
ROPE_THETA = 10000.0
LOG2E = 1.4426950408889634
GRID_W = 64
NUM_BUCKETS = 32
MAX_DISTANCE = 128

D_MODEL = 2048
HEADS = 8
HEAD_DIM = 128
A_QK_DIM = 64
B_KV_HEADS = 2
C_Q_RANK = 512
C_KV_RANK = 256
C_NOPE = 128
C_ROPE = 64
X_HEADS = 4
MEM_TOKENS = 256

Z_AQ, Z_AK, Z_AV, Z_BQ, Z_BK, Z_BV, Z_CQA, Z_CKVA, Z_CKR = 0, 1024, 2048, 3072, 4096, 4352, 4608, 5120, 5376
Z_RAW_SPLIT = 5440
Z_GATES = 5632
Z_COLS = Z_GATES + 3 * D_MODEL

VMEM_LIMIT = 56 * 1024 * 1024


def _cparams(sem):
    return pltpu.CompilerParams(dimension_semantics=sem, vmem_limit_bytes=VMEM_LIMIT)


def _rms(x, g):
    return x * lax.rsqrt(jnp.mean(x * x, axis=-1, keepdims=True) + EPS) * g


def _swap32(x):
    lane = lax.broadcasted_iota(jnp.int32, x.shape, 1)
    return jnp.where((lane % 64) < 32, pltpu.roll(x, 96, 1), pltpu.roll(x, 32, 1))


def _rope(x, cos, sin_signed):
    return x * cos + _swap32(x) * sin_signed


def _rms_matmul_kernel(x_ref, g_ref, w_ref, cs_ref, o_ref, xn_ref):
    @pl.when(pl.program_id(1) == 0)
    def _():
        xn_ref[...] = _rms(x_ref[...].astype(F32), g_ref[...]).astype(BF16)

    acc = jnp.dot(xn_ref[...], w_ref[...], preferred_element_type=F32)
    o_ref[...] = (acc * cs_ref[...]).astype(o_ref.dtype)


def rms_matmul(x, g, w, colscale, tm, tn, out_dtype=BF16):
    m, k = x.shape
    n = w.shape[1]
    return pl.pallas_call(
        _rms_matmul_kernel,
        grid=(m // tm, n // tn),
        in_specs=[
            pl.BlockSpec((tm, k), lambda i, j: (i, 0)),
            pl.BlockSpec((1, k), lambda i, j: (0, 0)),
            pl.BlockSpec((k, tn), lambda i, j: (0, j)),
            pl.BlockSpec((1, tn), lambda i, j: (0, j)),
        ],
        out_specs=pl.BlockSpec((tm, tn), lambda i, j: (i, j)),
        out_shape=jax.ShapeDtypeStruct((m, n), out_dtype),
        scratch_shapes=[pltpu.VMEM((tm, k), BF16)],
        compiler_params=_cparams(("parallel", "arbitrary")),
        name="rms_matmul",
    )(x, g, w, colscale)


def _prep_kernel(bq_ref, bkv_ref, cqa_ref, ckk_ref, cos_a_ref, sin_a_ref, cos_r_ref, sin_r_ref,
                 gbq_ref, gbk_ref, gcq_ref, gckv_ref, wq_ref, wkv_ref,
                 qb_out, kb_out, cq_out, ckv_out, kr_out, *, b_scale, c_scale):
    cos_a, sin_a = cos_a_ref[...], sin_a_ref[...]
    cos_r, sin_r = cos_r_ref[...], sin_r_ref[...]

    for h in range(HEADS):
        sl = slice(h * HEAD_DIM, (h + 1) * HEAD_DIM)
        y = _rms(bq_ref[:, sl].astype(F32), gbq_ref[...])
        qb_out[:, sl] = (_rope(y, cos_a, sin_a) * b_scale).astype(BF16)
    for n in range(B_KV_HEADS):
        sl = slice(n * HEAD_DIM, (n + 1) * HEAD_DIM)
        y = _rms(bkv_ref[:, sl].astype(F32), gbk_ref[...])
        kb_out[:, sl] = _rope(y, cos_a, sin_a).astype(BF16)

    xq = _rms(cqa_ref[...].astype(F32), gcq_ref[...]).astype(BF16)
    cq = jnp.dot(xq, wq_ref[...], preferred_element_type=F32)
    nope_cols = HEADS * C_NOPE
    cq_out[:, :nope_cols] = (cq[:, :nope_cols] * c_scale).astype(BF16)
    for h in range(HEADS):
        sl = slice(nope_cols + h * HEAD_DIM, nope_cols + (h + 1) * HEAD_DIM)
        cq_out[:, sl] = (_rope(cq[:, sl], cos_r, sin_r) * c_scale).astype(BF16)

    xkv = _rms(ckk_ref[:, :C_KV_RANK].astype(F32), gckv_ref[...]).astype(BF16)
    ckv_out[...] = jnp.dot(xkv, wkv_ref[...], preferred_element_type=F32).astype(BF16)
    kr = ckk_ref[:, C_KV_RANK:C_KV_RANK + HEAD_DIM].astype(F32)
    kr_out[...] = _rope(kr, cos_r, sin_r).astype(BF16)


def mixer_prep(z, tables, pos_block, gbq, gbk, gcq, gckv, wq, wkv, tm):
    t = z.shape[0]
    cos_a, sin_a, cos_r, sin_r = tables
    row = lambda w: pl.BlockSpec((tm, w), lambda i: (i, 0))
    tab = pl.BlockSpec((tm, HEAD_DIM), lambda i: (pos_block(i), 0))
    full = lambda a: pl.BlockSpec(a.shape, lambda i: (0,) * a.ndim)
    kern = functools.partial(_prep_kernel, b_scale=HEAD_DIM ** -0.5 * LOG2E,
                             c_scale=(C_NOPE + C_ROPE) ** -0.5 * LOG2E)
    return pl.pallas_call(
        kern,
        grid=(t // tm,),
        in_specs=[
            pl.BlockSpec((tm, 1024), lambda i: (i, Z_BQ // 1024)),
            pl.BlockSpec((tm, 512), lambda i: (i, Z_BK // 512)),
            pl.BlockSpec((tm, 512), lambda i: (i, Z_CQA // 512)),
            pl.BlockSpec((tm, 512), lambda i: (i, Z_CKVA // 512)),
            tab, tab, tab, tab,
            full(gbq), full(gbk), full(gcq), full(gckv), full(wq), full(wkv),
        ],
        out_specs=[row(1024), row(256), row(2048), row(2048), row(HEAD_DIM)],
        out_shape=[
            jax.ShapeDtypeStruct((t, 1024), BF16),
            jax.ShapeDtypeStruct((t, 256), BF16),
            jax.ShapeDtypeStruct((t, 2048), BF16),
            jax.ShapeDtypeStruct((t, 2048), BF16),
            jax.ShapeDtypeStruct((t, HEAD_DIM), BF16),
        ],
        compiler_params=_cparams(("parallel",)),
        name="mixer_prep",
    )(z, z, z, z, cos_a, sin_a, cos_r, sin_r, gbq, gbk, gcq, gckv, wq, wkv)


def _flash(q, k_refs, v_ref, s_refs, nk, tk, chunk_fn=None, tile_bias_fn=None, const_bias_fn=None):
    m_rows = q.shape[0]
    dv = v_ref.shape[1]
    ones = jnp.ones((tk, dv), BF16)

    def rows_of(ref, chunk):
        if isinstance(chunk, int):
            return ref[chunk * tk:(chunk + 1) * tk, :]
        return ref[pl.ds(pl.multiple_of(chunk * tk, tk), tk), :]

    def scores(chunk):
        ks = [rows_of(r, chunk) for r in k_refs]
        k = ks[0] if len(ks) == 1 else jnp.concatenate(ks, axis=1)
        return lax.dot_general(q, k, (((1,), (1,)), ((), ())), preferred_element_type=F32)

    def update(pos, chunk, s, carry):
        m, acc = carry
        sb = None if tile_bias_fn is None else tile_bias_fn(pos, chunk, s)
        if sb is not None:
            s = sb
            m_new = jnp.maximum(m, jnp.max(s, axis=-1, keepdims=True))
            shift = m_new
        elif const_bias_fn is not None:
            c = const_bias_fn(pos, chunk)
            m_new = jnp.maximum(m, jnp.max(s, axis=-1, keepdims=True) + c)
            shift = m_new - c
        else:
            m_new = jnp.maximum(m, jnp.max(s, axis=-1, keepdims=True))
            shift = m_new
        p = jnp.exp2((s - shift).astype(BF16))
        alpha = jnp.exp2(m - m_new)
        v_aug = jnp.concatenate([rows_of(v_ref, chunk), ones], axis=1)
        pv = jnp.dot(p, v_aug, preferred_element_type=F32)
        return m_new, alpha * acc + pv

    chunks = [pos if chunk_fn is None else chunk_fn(pos) for pos in range(nk)]
    carry = (jnp.full((m_rows, 1), -jnp.inf, F32), jnp.zeros((m_rows, 2 * dv), F32))
    s_refs[0][...] = scores(chunks[0])
    for pos in range(nk):
        if pos + 1 < nk:
            s_refs[(pos + 1) % 2][...] = scores(chunks[pos + 1])
        carry = update(pos, chunks[pos], s_refs[pos % 2][...], carry)
    m, acc = carry
    return m, acc[:, dv:], acc[:, :dv]


def _attn_kernel(*refs, n_q, n_k, nk, tk, streams):
    n_s = 2 * streams
    q_refs, k_refs = refs[:n_q], refs[n_q:n_q + n_k]
    v_ref, o_ref, s_refs = refs[n_q + n_k], refs[-n_s - 1], refs[-n_s:]
    rows = o_ref.shape[0] // streams
    for t in range(streams):
        sl = slice(t * rows, (t + 1) * rows)
        qs = [r[sl, :] for r in q_refs]
        q = qs[0] if n_q == 1 else jnp.concatenate(qs, axis=1)
        _, l, acc = _flash(q, k_refs, v_ref, s_refs[2 * t:2 * t + 2], nk, tk)
        o_ref[sl, :] = (acc / l).astype(o_ref.dtype)


def _attn_a_kernel(lam_ref, far_ref, q_ref, k_ref, v_ref, bias_ref, g_ref, *rest, tq, tk, nk, r, post_scale):
    o_ref, s_refs = rest[-3], rest[-2:]
    h, qi = pl.program_id(1), pl.program_id(2)
    q = q_ref[...]
    lane = lax.broadcasted_iota(jnp.int32, q.shape, 1)
    zero = jnp.zeros_like(q)
    q2 = jnp.concatenate([jnp.where(lane < A_QK_DIM, q, zero), jnp.where(lane >= A_QK_DIM, q, zero)], axis=0)

    diag = qi // r
    near = (0, 1, nk - 1)

    def chunk_fn(pos):
        return (diag + pos) % nk

    def tile_bias_fn(pos, chunk, s):
        if pos not in near:
            return None
        b = bias_ref[0, jnp.clip(chunk * r - qi, -r - 1, 2) + r + 1]
        return jnp.concatenate([s[:tq] + b, s[tq:] + b], axis=0)

    def const_bias_fn(pos, chunk):
        return jnp.where(chunk > diag, far_ref[h, 1], far_ref[h, 0])

    _, l, acc = _flash(q2, [k_ref], v_ref, s_refs, nk, tk, chunk_fn, tile_bias_fn, const_bias_fn)
    o = acc / l
    d = o[:tq] - lam_ref[0] * o[tq:]
    o_ref[...] = (_rms(d, g_ref[...]) * post_scale).astype(o_ref.dtype)


def attention(q_parts, k_parts, v_part, nseq, seq, tq, tk, streams):
    rows = tq * streams
    nq = seq // rows
    in_specs, args = [], []
    for a, cf in q_parts:
        in_specs.append(pl.BlockSpec((rows, HEAD_DIM), lambda s, h, i, cf=cf: (s * nq + i, cf(h))))
        args.append(a)
    for a, cf in list(k_parts) + [v_part]:
        in_specs.append(pl.BlockSpec((seq, HEAD_DIM), lambda s, h, i, cf=cf: (s, cf(h))))
        args.append(a)
    kern = functools.partial(_attn_kernel, n_q=len(q_parts), n_k=len(k_parts), nk=seq // tk, tk=tk,
                             streams=streams)
    return pl.pallas_call(
        kern,
        grid=(nseq, HEADS, nq),
        in_specs=in_specs,
        out_specs=pl.BlockSpec((rows, HEAD_DIM), lambda s, h, i: (s * nq + i, h)),
        out_shape=jax.ShapeDtypeStruct((nseq * seq, HEADS * HEAD_DIM), BF16),
        scratch_shapes=[pltpu.VMEM((tq, tk), F32)] * (2 * streams),
        compiler_params=_cparams(("parallel", "parallel", "arbitrary")),
        name="attention",
    )(*args)


def attention_a(lam, z, bias, g, nseq, seq, tq, tk, post_scale):
    nq = seq // tq
    r = tk // tq
    far = bias[:, :: r + 3, 0, 0]
    in_specs = [
        pl.BlockSpec(memory_space=pltpu.SMEM),
        pl.BlockSpec(memory_space=pltpu.SMEM),
        pl.BlockSpec((tq, HEAD_DIM), lambda s, h, i: (s * nq + i, Z_AQ // HEAD_DIM + h)),
        pl.BlockSpec((seq, HEAD_DIM), lambda s, h, i: (s, Z_AK // HEAD_DIM + h)),
        pl.BlockSpec((seq, HEAD_DIM), lambda s, h, i: (s, Z_AV // HEAD_DIM + h)),
        pl.BlockSpec((1, r + 4, tq, tk), lambda s, h, i: (h, 0, 0, 0)),
        pl.BlockSpec((1, HEAD_DIM), lambda s, h, i: (0, 0)),
    ]
    kern = functools.partial(_attn_a_kernel, tq=tq, tk=tk, nk=seq // tk, r=r, post_scale=post_scale)
    return pl.pallas_call(
        kern,
        grid=(nseq, HEADS, nq),
        in_specs=in_specs,
        out_specs=pl.BlockSpec((tq, HEAD_DIM), lambda s, h, i: (s * nq + i, h)),
        out_shape=jax.ShapeDtypeStruct((nseq * seq, HEADS * HEAD_DIM), BF16),
        scratch_shapes=[pltpu.VMEM((2 * tq, tk), F32)] * 2,
        compiler_params=_cparams(("parallel", "parallel", "arbitrary")),
        name="attention_a",
    )(lam, far, z, z, z, bias, g)


def _merge_kernel(oa_ref, ob_ref, oc_ref, wa_ref, wb_ref, wc_ref, ga_ref, gb_ref, gc_ref, o_ref):
    def branch(o_r, w_r, g_r):
        y = jnp.dot(o_r[...], w_r[...], preferred_element_type=F32)
        return jax.nn.sigmoid(g_r[...].astype(F32)) * y

    merged = branch(oa_ref, wa_ref, ga_ref) + branch(ob_ref, wb_ref, gb_ref) + branch(oc_ref, wc_ref, gc_ref)
    o_ref[...] = merged.astype(o_ref.dtype)


def gated_merge(oa, ob, oc, wa, wb, wc, z, tm, tn):
    t, k = oa.shape
    n = wa.shape[1]
    o_spec = pl.BlockSpec((tm, k), lambda i, j: (i, 0))
    w_spec = pl.BlockSpec((k, tn), lambda i, j: (0, j))
    gate = lambda b: pl.BlockSpec((tm, tn), lambda i, j: (i, (Z_GATES + b * n) // tn + j))
    return pl.pallas_call(
        _merge_kernel,
        grid=(t // tm, n // tn),
        in_specs=[o_spec, o_spec, o_spec, w_spec, w_spec, w_spec, gate(0), gate(1), gate(2)],
        out_specs=pl.BlockSpec((tm, tn), lambda i, j: (i, j)),
        out_shape=jax.ShapeDtypeStruct((t, n), BF16),
        compiler_params=_cparams(("parallel", "arbitrary")),
        name="gated_merge",
    )(oa, ob, oc, wa, wb, wc, z, z, z)


def _mm_postnorm_kernel(a_ref, w_ref, g_ref, x_ref, o_ref):
    y = jnp.dot(a_ref[...], w_ref[...], preferred_element_type=F32)
    o_ref[...] = x_ref[...] + _rms(y, g_ref[...])


def mm_postnorm_residual(a, w, g, x, tm):
    t, k = a.shape
    n = w.shape[1]
    return pl.pallas_call(
        _mm_postnorm_kernel,
        grid=(t // tm,),
        in_specs=[
            pl.BlockSpec((tm, k), lambda i: (i, 0)),
            pl.BlockSpec((k, n), lambda i: (0, 0)),
            pl.BlockSpec((1, n), lambda i: (0, 0)),
            pl.BlockSpec((tm, n), lambda i: (i, 0)),
        ],
        out_specs=pl.BlockSpec((tm, n), lambda i: (i, 0)),
        out_shape=jax.ShapeDtypeStruct((t, n), F32),
        compiler_params=_cparams(("parallel",)),
        name="mm_postnorm_residual",
    )(a, w, g, x)


def _xattn_kernel(x_ref, gpre_ref, wq_ref, kv_ref, wo_ref, gpost_ref, o_ref, *, q_scale):
    x = x_ref[...]
    h = _rms(x, gpre_ref[...]).astype(BF16)
    q = (jnp.dot(h, wq_ref[...], preferred_element_type=F32) * q_scale).astype(BF16)
    kv_cols = X_HEADS * HEAD_DIM
    outs = []
    for hd in range(X_HEADS):
        sl = slice(hd * HEAD_DIM, (hd + 1) * HEAD_DIM)
        kh = kv_ref[0, :, sl]
        vh = kv_ref[0, :, kv_cols + hd * HEAD_DIM:kv_cols + (hd + 1) * HEAD_DIM]
        s = lax.dot_general(q[:, sl], kh, (((1,), (1,)), ((), ())), preferred_element_type=F32)
        p = jnp.exp2(s - jnp.max(s, axis=-1, keepdims=True))
        l = jnp.sum(p, axis=-1, keepdims=True)
        outs.append((jnp.dot(p.astype(BF16), vh, preferred_element_type=F32) / l).astype(BF16))
    o = jnp.concatenate(outs, axis=1)
    y = jnp.dot(o, wo_ref[...], preferred_element_type=F32)
    o_ref[...] = x + _rms(y, gpost_ref[...])


def memory_xattn(x, gpre, wq, memkv, mem_block, wo, gpost, tm):
    t, d = x.shape
    full = lambda a: pl.BlockSpec(a.shape, lambda i: (0,) * a.ndim)
    kern = functools.partial(_xattn_kernel, q_scale=HEAD_DIM ** -0.5 * LOG2E)
    return pl.pallas_call(
        kern,
        grid=(t // tm,),
        in_specs=[
            pl.BlockSpec((tm, d), lambda i: (i, 0)),
            full(gpre), full(wq),
            pl.BlockSpec((1,) + memkv.shape[1:], lambda i: (mem_block(i), 0, 0)),
            full(wo), full(gpost),
        ],
        out_specs=pl.BlockSpec((tm, d), lambda i: (i, 0)),
        out_shape=jax.ShapeDtypeStruct((t, d), F32),
        compiler_params=_cparams(("parallel",)),
        name="memory_xattn",
    )(x, gpre, wq, memkv, wo, gpost)


def _ffn_kernel(x_ref, gpre_ref, wg_ref, wu_ref, wd_ref, gpost_ref, o_ref, h_ref, acc_ref):
    f = pl.program_id(1)

    @pl.when(f == 0)
    def _():
        h_ref[...] = _rms(x_ref[...], gpre_ref[...]).astype(BF16)
        acc_ref[...] = jnp.zeros_like(acc_ref)

    h = h_ref[...]
    a = jnp.dot(h, wg_ref[...], preferred_element_type=F32)
    b = jnp.dot(h, wu_ref[...], preferred_element_type=F32)
    t = (a * jax.nn.sigmoid(a) * b).astype(BF16)
    acc_ref[...] += jnp.dot(t, wd_ref[...], preferred_element_type=F32)

    @pl.when(f == pl.num_programs(1) - 1)
    def _():
        o_ref[...] = x_ref[...] + _rms(acc_ref[...], gpost_ref[...])


def swiglu_ffn(x, gpre, wg, wu, wd, gpost, tm, tf):
    t, d = x.shape
    ff = wg.shape[1]
    return pl.pallas_call(
        _ffn_kernel,
        grid=(t // tm, ff // tf),
        in_specs=[
            pl.BlockSpec((tm, d), lambda i, f: (i, 0)),
            pl.BlockSpec((1, d), lambda i, f: (0, 0)),
            pl.BlockSpec((d, tf), lambda i, f: (0, f)),
            pl.BlockSpec((d, tf), lambda i, f: (0, f)),
            pl.BlockSpec((tf, d), lambda i, f: (f, 0)),
            pl.BlockSpec((1, d), lambda i, f: (0, 0)),
        ],
        out_specs=pl.BlockSpec((tm, d), lambda i, f: (i, 0)),
        out_shape=jax.ShapeDtypeStruct((t, d), F32),
        scratch_shapes=[pltpu.VMEM((tm, d), BF16), pltpu.VMEM((tm, d), F32)],
        compiler_params=_cparams(("parallel", "arbitrary")),
        name="swiglu_ffn",
    )(x, gpre, wg, wu, wd, gpost)


def _rotary_tables(seq):
    pos = jnp.arange(seq)

    def cs(p):
        inv = ROPE_THETA ** (-jnp.arange(0, 64, 2, dtype=F32) / 64)
        ang = p.astype(F32)[:, None] * inv[None, :]
        c, s = jnp.cos(ang), jnp.sin(ang)
        return jnp.concatenate([c, c], axis=-1), jnp.concatenate([-s, s], axis=-1)

    c_row, s_row = cs(pos // GRID_W)
    c_col, s_col = cs(pos % GRID_W)
    c_tok, s_tok = cs(pos)
    return (jnp.concatenate([c_row, c_col], axis=-1), jnp.concatenate([s_row, s_col], axis=-1),
            jnp.concatenate([c_tok, c_tok], axis=-1), jnp.concatenate([s_tok, s_tok], axis=-1))


def _t5_bucket(rel):
    nb = NUM_BUCKETS // 2
    max_exact = nb // 2
    ret = (rel > 0).astype(jnp.int32) * nb
    n = jnp.abs(rel)
    nf = jnp.maximum(n, 1).astype(F32)
    large = max_exact + (jnp.log(nf / max_exact) / math.log(MAX_DISTANCE / max_exact)
                         * (nb - max_exact)).astype(jnp.int32)
    large = jnp.minimum(large, nb - 1)
    return ret + jnp.where(n < max_exact, n, large)


def _bias_kernel(tab_ref, o_ref, *, tq, tk, r):
    h = pl.program_id(1)
    rel = ((pl.program_id(0) - (r + 1)) * tq + lax.broadcasted_iota(jnp.int32, (tq, tk), 1)
           - lax.broadcasted_iota(jnp.int32, (tq, tk), 0))
    bucket = _t5_bucket(rel)
    v = jnp.full((tq, tk), tab_ref[h, 0], F32)
    for b in range(1, NUM_BUCKETS):
        v = jnp.where(bucket == b, tab_ref[h, b], v)
    o_ref[0, 0] = v * LOG2E


def _bias_tiles(rel_bias, tq, tk):
    assert tq >= 128 and tk % tq == 0
    r = tk // tq
    return pl.pallas_call(
        functools.partial(_bias_kernel, tq=tq, tk=tk, r=r),
        grid=(r + 4, HEADS),
        in_specs=[pl.BlockSpec(memory_space=pltpu.SMEM)],
        out_specs=pl.BlockSpec((1, 1, tq, tk), lambda t, h: (h, t, 0, 0)),
        out_shape=jax.ShapeDtypeStruct((HEADS, r + 4, tq, tk), F32),
        compiler_params=_cparams(("parallel", "parallel")),
        name="bias_tiles",
    )(rel_bias.T.astype(F32))


def kernel(x_prompt, x_sample, mem_prompt, mem_sample, rel_bias, g_mix_pre, g_mix_post, w_in, lam_q1, lam_k1, lam_q2, lam_k2, g_a_out, g_b_q, g_b_k, g_c_q, g_c_kv, w_c_q_up, w_c_kv_up, w_br_a, w_br_b, w_br_c, w_mix_out, g_x_pre, g_x_post, g_mem, w_x_q, w_x_kv, w_x_out, g_ffn_pre, g_ffn_post, w_ffn_gate, w_ffn_up, w_ffn_down):
    depth = w_in.shape[0]
    d = x_prompt.shape[-1]
    tm_row = 512
    tq_a, tk_a = 256, 1024
    tq, tk, streams = 512, 1024, 1

    groups = [(x.reshape(-1, d), m, x.shape[0], x.shape[1])
              for x, m in ((x_prompt, mem_prompt), (x_sample, mem_sample))]
    tables = _rotary_tables(max(g[3] for g in groups))
    bias = _bias_tiles(rel_bias, tq_a, tk_a)
    row = lambda g: g.reshape(1, -1).astype(F32)
    in_scale = jnp.concatenate([jnp.full((1, Z_AK), A_QK_DIM ** -0.5 * LOG2E, F32),
                                jnp.ones((1, Z_COLS - Z_AK), F32)], axis=-1)
    ones_kv = jnp.ones((1, w_x_kv.shape[-1]), F32)
    kv_group = HEADS // B_KV_HEADS

    xs = [g[0] for g in groups]
    for l in range(depth):
        lam_init = 0.8 - 0.6 * math.exp(-0.3 * l)
        lam = (jnp.exp(jnp.sum(lam_q1[l].astype(F32) * lam_k1[l].astype(F32)))
               - jnp.exp(jnp.sum(lam_q2[l].astype(F32) * lam_k2[l].astype(F32))) + lam_init).reshape(1)

        w_in_l = jnp.concatenate(
            [w_in[l, :, :Z_RAW_SPLIT].astype(BF16), jnp.zeros((d, Z_GATES - Z_RAW_SPLIT), BF16),
             w_in[l, :, Z_RAW_SPLIT:].astype(BF16)], axis=-1)
        wq3 = w_c_q_up[l].reshape(C_Q_RANK, HEADS, C_NOPE + C_ROPE)
        w_cq = jnp.concatenate(
            [wq3[..., :C_NOPE].reshape(C_Q_RANK, HEADS * C_NOPE),
             jnp.pad(wq3[..., C_NOPE:], ((0, 0), (0, 0), (0, HEAD_DIM - C_ROPE))).reshape(
                 C_Q_RANK, HEADS * HEAD_DIM)], axis=-1).astype(BF16)
        w_ckv = w_c_kv_up[l].astype(BF16)
        w_a, w_b, w_c = w_br_a[l].astype(BF16), w_br_b[l].astype(BF16), w_br_c[l].astype(BF16)
        w_mix = w_mix_out[l].astype(BF16)
        w_xq, w_xkv, w_xo = w_x_q[l].astype(BF16), w_x_kv[l].astype(BF16), w_x_out[l].astype(BF16)
        w_fg, w_fu, w_fd = w_ffn_gate[l].astype(BF16), w_ffn_up[l].astype(BF16), w_ffn_down[l].astype(BF16)

        for gi, (_, mem, nseq, seq) in enumerate(groups):
            x = xs[gi]
            blocks_per_seq = seq // tm_row
            z = rms_matmul(x, row(g_mix_pre[l]), w_in_l, in_scale, tm=1024, tn=512)
            qb, kb, cq, ckv, kr = mixer_prep(z, tables, lambda i: i % blocks_per_seq, row(g_b_q[l]), row(g_b_k[l]),
                                             row(g_c_q[l]), row(g_c_kv[l]), w_cq, w_ckv, tm_row)
            oa = attention_a(lam, z, bias, row(g_a_out[l]), nseq, seq, tq_a, tk_a, 1.0 - lam_init)
            ob = attention([(qb, lambda h: h)],
                           [(kb, lambda h: h // kv_group)],
                           (z, lambda h: Z_BV // HEAD_DIM + h // kv_group),
                           nseq, seq, tq, tk, streams)
            oc = attention([(cq, lambda h: h), (cq, lambda h: HEADS + h)],
                           [(ckv, lambda h: 2 * h), (kr, lambda h: 0)],
                           (ckv, lambda h: 2 * h + 1),
                           nseq, seq, tq, tk, streams)
            merged = gated_merge(oa, ob, oc, w_a, w_b, w_c, z, tm=1024, tn=512)
            x = mm_postnorm_residual(merged, w_mix, row(g_mix_post[l]), x, tm=tm_row)

            mem2 = mem.reshape(nseq * MEM_TOKENS, d)
            memkv = rms_matmul(mem2, row(g_mem[l]), w_xkv, ones_kv, tm=mem2.shape[0], tn=512)
            memkv = memkv.reshape(nseq, MEM_TOKENS, -1)
            x = memory_xattn(x, row(g_x_pre[l]), w_xq, memkv, lambda i: i // blocks_per_seq, w_xo,
                             row(g_x_post[l]), tm_row)
            xs[gi] = swiglu_ffn(x, row(g_ffn_pre[l]), w_fg, w_fu, w_fd, row(g_ffn_post[l]), tm=tm_row, tf=512)

    return tuple(x.reshape(g[2], g[3], d) for x, g in zip(xs, groups))
```

```python
import functools
import math

import jax
import jax.numpy as jnp
import numpy as np
from jax import lax
from jax.experimental import pallas as pl
from jax.experimental.pallas import tpu as pltpu

F32 = jnp.float32
BF16 = jnp.bfloat16

EPS = 1e-6
ROPE_THETA = 10000.0
LOG2E = 1.4426950408889634
GRID_W = 64
NUM_BUCKETS = 32
MAX_DISTANCE = 128

D_MODEL = 2048
HEADS = 8
HEAD_DIM = 128
A_QK_DIM = 64
B_KV_HEADS = 2
C_Q_RANK = 512
C_KV_RANK = 256
C_NOPE = 128
C_ROPE = 64
X_HEADS = 4
MEM_TOKENS = 256

Z_AQ, Z_AK, Z_AV, Z_BQ, Z_BK, Z_BV, Z_CQA, Z_CKVA, Z_CKR = 0, 1024, 2048, 3072, 4096, 4352, 4608, 5120, 5376
Z_RAW_SPLIT = 5440
Z_GATES = 5632
Z_COLS = Z_GATES + 3 * D_MODEL

VMEM_LIMIT = 56 * 1024 * 1024


def _cparams(sem):
    return pltpu.CompilerParams(dimension_semantics=sem, vmem_limit_bytes=VMEM_LIMIT)


def _rms(x, g):
    return x * lax.rsqrt(jnp.mean(x * x, axis=-1, keepdims=True) + EPS) * g


def _swap32(x):
    lane = lax.broadcasted_iota(jnp.int32, x.shape, 1)
    return jnp.where((lane % 64) < 32, pltpu.roll(x, 96, 1), pltpu.roll(x, 32, 1))


def _rope(x, cos, sin_signed):
    return x * cos + _swap32(x) * sin_signed


def _rms_matmul_kernel(x_ref, g_ref, w_ref, cs_ref, o_ref, xn_ref):
    @pl.when(pl.program_id(1) == 0)
    def _():
        xn_ref[...] = _rms(x_ref[...].astype(F32), g_ref[...]).astype(BF16)

    acc = jnp.dot(xn_ref[...], w_ref[...], preferred_element_type=F32)
    o_ref[...] = (acc * cs_ref[...]).astype(o_ref.dtype)


def rms_matmul(x, g, w, layer, colscale, tm, tn, out_dtype=BF16):
    m, k = x.shape
    n = w.shape[2]
    return pl.pallas_call(
        _rms_matmul_kernel,
        grid=(m // tm, n // tn),
        in_specs=[
            pl.BlockSpec((tm, k), lambda i, j: (i, 0)),
            pl.BlockSpec((1, k), lambda i, j: (0, 0)),
            pl.BlockSpec((None, k, tn), lambda i, j: (layer, 0, j)),
            pl.BlockSpec((1, tn), lambda i, j: (0, j)),
        ],
        out_specs=pl.BlockSpec((tm, tn), lambda i, j: (i, j)),
        out_shape=jax.ShapeDtypeStruct((m, n), out_dtype),
        scratch_shapes=[pltpu.VMEM((tm, k), BF16)],
        compiler_params=_cparams(("parallel", "arbitrary")),
        name="rms_matmul",
    )(x, g, w, colscale)


def _w_in_prep_kernel(lo_ref, hi_ref, o_ref, *, split_block, split_lane):
    j = pl.program_id(2)

    @pl.when(j < split_block)
    def _():
        o_ref[...] = hi_ref[...].astype(BF16)

    @pl.when(j == split_block)
    def _():
        lane = lax.broadcasted_iota(jnp.int32, hi_ref.shape, 1)
        o_ref[...] = jnp.where(lane < split_lane, hi_ref[...], 0.0).astype(BF16)

    @pl.when(j > split_block)
    def _():
        o_ref[...] = jnp.concatenate([lo_ref[:, split_lane:], hi_ref[:, :split_lane]], axis=1).astype(BF16)


def w_in_prep(w_in, tr, tn):
    depth, d, raw = w_in.shape
    pad = Z_GATES - Z_RAW_SPLIT
    assert Z_COLS == raw + pad and pad < tn
    split_block, split_lane = divmod(Z_RAW_SPLIT, tn)
    last = (raw - 1) // tn
    return pl.pallas_call(
        functools.partial(_w_in_prep_kernel, split_block=split_block, split_lane=split_lane),
        grid=(depth, d // tr, Z_COLS // tn),
        in_specs=[
            pl.BlockSpec((None, tr, tn), lambda l, i, j: (l, i, jnp.maximum(j - 1, 0))),
            pl.BlockSpec((None, tr, tn), lambda l, i, j: (l, i, jnp.minimum(j, last))),
        ],
        out_specs=pl.BlockSpec((None, tr, tn), lambda l, i, j: (l, i, j)),
        out_shape=jax.ShapeDtypeStruct((depth, d, Z_COLS), BF16),
        compiler_params=_cparams(("parallel", "parallel", "arbitrary")),
        name="w_in_prep",
    )(w_in, w_in)


def _prep_kernel(bq_ref, bkv_ref, cqa_ref, ckk_ref, cos_a_ref, sin_a_ref, cos_r_ref, sin_r_ref,
                 gbq_ref, gbk_ref, gcq_ref, gckv_ref, wq_ref, wkv_ref,
                 qb_out, kb_out, cq_out, ckv_out, kr_out, *, b_scale, c_scale):
    cos_a, sin_a = cos_a_ref[...], sin_a_ref[...]
    cos_r, sin_r = cos_r_ref[...], sin_r_ref[...]

    for h in range(HEADS):
        sl = slice(h * HEAD_DIM, (h + 1) * HEAD_DIM)
        y = _rms(bq_ref[:, sl].astype(F32), gbq_ref[...])
        qb_out[:, sl] = (_rope(y, cos_a, sin_a) * b_scale).astype(BF16)
    for n in range(B_KV_HEADS):
        sl = slice(n * HEAD_DIM, (n + 1) * HEAD_DIM)
        y = _rms(bkv_ref[:, sl].astype(F32), gbk_ref[...])
        kb_out[:, sl] = _rope(y, cos_a, sin_a).astype(BF16)

    xq = _rms(cqa_ref[...].astype(F32), gcq_ref[...]).astype(BF16)
    cq = jnp.dot(xq, wq_ref[...], preferred_element_type=F32)
    nope_cols = HEADS * C_NOPE
    cq_out[:, :nope_cols] = (cq[:, :nope_cols] * c_scale).astype(BF16)
    for h in range(HEADS):
        sl = slice(nope_cols + h * HEAD_DIM, nope_cols + (h + 1) * HEAD_DIM)
        cq_out[:, sl] = (_rope(cq[:, sl], cos_r, sin_r) * c_scale).astype(BF16)

    xkv = _rms(ckk_ref[:, :C_KV_RANK].astype(F32), gckv_ref[...]).astype(BF16)
    ckv_out[...] = jnp.dot(xkv, wkv_ref[...], preferred_element_type=F32).astype(BF16)
    kr = ckk_ref[:, C_KV_RANK:C_KV_RANK + HEAD_DIM].astype(F32)
    kr_out[...] = _rope(kr, cos_r, sin_r).astype(BF16)


def mixer_prep(z, tables, pos_block, gbq, gbk, gcq, gckv, wq, wkv, tm):
    t = z.shape[0]
    cos_a, sin_a, cos_r, sin_r = tables
    row = lambda w: pl.BlockSpec((tm, w), lambda i: (i, 0))
    tab = pl.BlockSpec((tm, HEAD_DIM), lambda i: (pos_block(i), 0))
    full = lambda a: pl.BlockSpec(a.shape, lambda i: (0,) * a.ndim)
    kern = functools.partial(_prep_kernel, b_scale=HEAD_DIM ** -0.5 * LOG2E,
                             c_scale=(C_NOPE + C_ROPE) ** -0.5 * LOG2E)
    return pl.pallas_call(
        kern,
        grid=(t // tm,),
        in_specs=[
            pl.BlockSpec((tm, 1024), lambda i: (i, Z_BQ // 1024)),
            pl.BlockSpec((tm, 512), lambda i: (i, Z_BK // 512)),
            pl.BlockSpec((tm, 512), lambda i: (i, Z_CQA // 512)),
            pl.BlockSpec((tm, 512), lambda i: (i, Z_CKVA // 512)),
            tab, tab, tab, tab,
            full(gbq), full(gbk), full(gcq), full(gckv), full(wq), full(wkv),
        ],
        out_specs=[row(1024), row(256), row(2048), row(2048), row(HEAD_DIM)],
        out_shape=[
            jax.ShapeDtypeStruct((t, 1024), BF16),
            jax.ShapeDtypeStruct((t, 256), BF16),
            jax.ShapeDtypeStruct((t, 2048), BF16),
            jax.ShapeDtypeStruct((t, 2048), BF16),
            jax.ShapeDtypeStruct((t, HEAD_DIM), BF16),
        ],
        compiler_params=_cparams(("parallel",)),
        name="mixer_prep",
    )(z, z, z, z, cos_a, sin_a, cos_r, sin_r, gbq, gbk, gcq, gckv, wq, wkv)


def _flash(q, k_refs, v_ref, s_refs, nk, tk, chunk_fn=None, tile_bias_fn=None, const_bias_fn=None):
    m_rows = q.shape[0]
    dv = v_ref.shape[1]
    ones = jnp.ones((tk, dv), BF16)

    def rows_of(ref, chunk):
        if isinstance(chunk, int):
            return ref[chunk * tk:(chunk + 1) * tk, :]
        return ref[pl.ds(pl.multiple_of(chunk * tk, tk), tk), :]

    def scores(chunk):
        ks = [rows_of(r, chunk) for r in k_refs]
        k = ks[0] if len(ks) == 1 else jnp.concatenate(ks, axis=1)
        return lax.dot_general(q, k, (((1,), (1,)), ((), ())), preferred_element_type=F32)

    def update(pos, chunk, s, carry):
        m, acc = carry
        sb = None if tile_bias_fn is None else tile_bias_fn(pos, chunk, s)
        if sb is not None:
            s = sb
            m_new = jnp.maximum(m, jnp.max(s, axis=-1, keepdims=True))
            shift = m_new
        elif const_bias_fn is not None:
            c = const_bias_fn(pos, chunk)
            m_new = jnp.maximum(m, jnp.max(s, axis=-1, keepdims=True) + c)
            shift = m_new - c
        else:
            m_new = jnp.maximum(m, jnp.max(s, axis=-1, keepdims=True))
            shift = m_new
        p = jnp.exp2((s - shift).astype(BF16))
        alpha = jnp.exp2(m - m_new)
        v_aug = jnp.concatenate([rows_of(v_ref, chunk), ones], axis=1)
        pv = jnp.dot(p, v_aug, preferred_element_type=F32)
        return m_new, alpha * acc + pv

    chunks = [pos if chunk_fn is None else chunk_fn(pos) for pos in range(nk)]
    carry = (jnp.full((m_rows, 1), -jnp.inf, F32), jnp.zeros((m_rows, 2 * dv), F32))
    s_refs[0][...] = scores(chunks[0])
    for pos in range(nk):
        if pos + 1 < nk:
            s_refs[(pos + 1) % 2][...] = scores(chunks[pos + 1])
        carry = update(pos, chunks[pos], s_refs[pos % 2][...], carry)
    m, acc = carry
    return m, acc[:, dv:], acc[:, :dv]


def _attn_kernel(*refs, n_q, n_k, nk, tk, streams):
    n_s = 2 * streams
    q_refs, k_refs = refs[:n_q], refs[n_q:n_q + n_k]
    v_ref, o_ref, s_refs = refs[n_q + n_k], refs[-n_s - 1], refs[-n_s:]
    rows = o_ref.shape[0] // streams
    for t in range(streams):
        sl = slice(t * rows, (t + 1) * rows)
        qs = [r[sl, :] for r in q_refs]
        q = qs[0] if n_q == 1 else jnp.concatenate(qs, axis=1)
        _, l, acc = _flash(q, k_refs, v_ref, s_refs[2 * t:2 * t + 2], nk, tk)
        o_ref[sl, :] = (acc / l).astype(o_ref.dtype)


def _attn_a_kernel(lam_ref, far_ref, q_ref, k_ref, v_ref, bias_ref, g_ref, *rest, tq, tk, nk, r, post_scale):
    o_ref, s_refs = rest[-3], rest[-2:]
    h, qi = pl.program_id(1), pl.program_id(2)
    q = q_ref[...]
    lane = lax.broadcasted_iota(jnp.int32, q.shape, 1)
    zero = jnp.zeros_like(q)
    q2 = jnp.concatenate([jnp.where(lane < A_QK_DIM, q, zero), jnp.where(lane >= A_QK_DIM, q, zero)], axis=0)

    diag = qi // r
    near = (0, 1, nk - 1)

    def chunk_fn(pos):
        return (diag + pos) % nk

    def tile_bias_fn(pos, chunk, s):
        if pos not in near:
            return None
        b = bias_ref[0, jnp.clip(chunk * r - qi, -r - 1, 2) + r + 1]
        return jnp.concatenate([s[:tq] + b, s[tq:] + b], axis=0)

    def const_bias_fn(pos, chunk):
        return jnp.where(chunk > diag, far_ref[h, 1], far_ref[h, 0])

    _, l, acc = _flash(q2, [k_ref], v_ref, s_refs, nk, tk, chunk_fn, tile_bias_fn, const_bias_fn)
    o = acc / l
    d = o[:tq] - lam_ref[0] * o[tq:]
    o_ref[...] = (_rms(d, g_ref[...]) * post_scale).astype(o_ref.dtype)


def attention(q_parts, k_parts, v_part, nseq, seq, tq, tk, streams):
    rows = tq * streams
    nq = seq // rows
    in_specs, args = [], []
    for a, cf in q_parts:
        in_specs.append(pl.BlockSpec((rows, HEAD_DIM), lambda s, h, i, cf=cf: (s * nq + i, cf(h))))
        args.append(a)
    for a, cf in list(k_parts) + [v_part]:
        in_specs.append(pl.BlockSpec((seq, HEAD_DIM), lambda s, h, i, cf=cf: (s, cf(h))))
        args.append(a)
    kern = functools.partial(_attn_kernel, n_q=len(q_parts), n_k=len(k_parts), nk=seq // tk, tk=tk,
                             streams=streams)
    return pl.pallas_call(
        kern,
        grid=(nseq, HEADS, nq),
        in_specs=in_specs,
        out_specs=pl.BlockSpec((rows, HEAD_DIM), lambda s, h, i: (s * nq + i, h)),
        out_shape=jax.ShapeDtypeStruct((nseq * seq, HEADS * HEAD_DIM), BF16),
        scratch_shapes=[pltpu.VMEM((tq, tk), F32)] * (2 * streams),
        compiler_params=_cparams(("parallel", "parallel", "arbitrary")),
        name="attention",
    )(*args)


def attention_a(lam, z, bias, g, nseq, seq, tq, tk, post_scale):
    nq = seq // tq
    r = tk // tq
    far = bias[:, :: r + 3, 0, 0]
    in_specs = [
        pl.BlockSpec(memory_space=pltpu.SMEM),
        pl.BlockSpec(memory_space=pltpu.SMEM),
        pl.BlockSpec((tq, HEAD_DIM), lambda s, h, i: (s * nq + i, Z_AQ // HEAD_DIM + h)),
        pl.BlockSpec((seq, HEAD_DIM), lambda s, h, i: (s, Z_AK // HEAD_DIM + h)),
        pl.BlockSpec((seq, HEAD_DIM), lambda s, h, i: (s, Z_AV // HEAD_DIM + h)),
        pl.BlockSpec((1, r + 4, tq, tk), lambda s, h, i: (h, 0, 0, 0)),
        pl.BlockSpec((1, HEAD_DIM), lambda s, h, i: (0, 0)),
    ]
    kern = functools.partial(_attn_a_kernel, tq=tq, tk=tk, nk=seq // tk, r=r, post_scale=post_scale)
    return pl.pallas_call(
        kern,
        grid=(nseq, HEADS, nq),
        in_specs=in_specs,
        out_specs=pl.BlockSpec((tq, HEAD_DIM), lambda s, h, i: (s * nq + i, h)),
        out_shape=jax.ShapeDtypeStruct((nseq * seq, HEADS * HEAD_DIM), BF16),
        scratch_shapes=[pltpu.VMEM((2 * tq, tk), F32)] * 2,
        compiler_params=_cparams(("parallel", "parallel", "arbitrary")),
        name="attention_a",
    )(lam, far, z, z, z, bias, g)


def _merge_kernel(oa_ref, ob_ref, oc_ref, wa_ref, wb_ref, wc_ref, ga_ref, gb_ref, gc_ref, o_ref):
    def branch(o_r, w_r, g_r):
        y = jnp.dot(o_r[...], w_r[...], preferred_element_type=F32)
        return jax.nn.sigmoid(g_r[...].astype(F32)) * y

    merged = branch(oa_ref, wa_ref, ga_ref) + branch(ob_ref, wb_ref, gb_ref) + branch(oc_ref, wc_ref, gc_ref)
    o_ref[...] = merged.astype(o_ref.dtype)


def gated_merge(oa, ob, oc, wa, wb, wc, z, tm, tn):
    t, k = oa.shape
    n = wa.shape[1]
    o_spec = pl.BlockSpec((tm, k), lambda i, j: (i, 0))
    w_spec = pl.BlockSpec((k, tn), lambda i, j: (0, j))
    gate = lambda b: pl.BlockSpec((tm, tn), lambda i, j: (i, (Z_GATES + b * n) // tn + j))
    return pl.pallas_call(
        _merge_kernel,
        grid=(t // tm, n // tn),
        in_specs=[o_spec, o_spec, o_spec, w_spec, w_spec, w_spec, gate(0), gate(1), gate(2)],
        out_specs=pl.BlockSpec((tm, tn), lambda i, j: (i, j)),
        out_shape=jax.ShapeDtypeStruct((t, n), BF16),
        compiler_params=_cparams(("parallel", "arbitrary")),
        name="gated_merge",
    )(oa, ob, oc, wa, wb, wc, z, z, z)


def _mm_postnorm_kernel(a_ref, w_ref, g_ref, x_ref, o_ref):
    y = jnp.dot(a_ref[...], w_ref[...], preferred_element_type=F32)
    o_ref[...] = x_ref[...] + _rms(y, g_ref[...])


def mm_postnorm_residual(a, w, g, x, tm):
    t, k = a.shape
    n = w.shape[1]
    return pl.pallas_call(
        _mm_postnorm_kernel,
        grid=(t // tm,),
        in_specs=[
            pl.BlockSpec((tm, k), lambda i: (i, 0)),
            pl.BlockSpec((k, n), lambda i: (0, 0)),
            pl.BlockSpec((1, n), lambda i: (0, 0)),
            pl.BlockSpec((tm, n), lambda i: (i, 0)),
        ],
        out_specs=pl.BlockSpec((tm, n), lambda i: (i, 0)),
        out_shape=jax.ShapeDtypeStruct((t, n), F32),
        compiler_params=_cparams(("parallel",)),
        name="mm_postnorm_residual",
    )(a, w, g, x)


def _xattn_kernel(x_ref, gpre_ref, wq_ref, kv_ref, wo_ref, gpost_ref, o_ref, *, q_scale):
    x = x_ref[...]
    h = _rms(x, gpre_ref[...]).astype(BF16)
    q = (jnp.dot(h, wq_ref[...], preferred_element_type=F32) * q_scale).astype(BF16)
    kv_cols = X_HEADS * HEAD_DIM
    outs = []
    for hd in range(X_HEADS):
        sl = slice(hd * HEAD_DIM, (hd + 1) * HEAD_DIM)
        kh = kv_ref[0, :, sl]
        vh = kv_ref[0, :, kv_cols + hd * HEAD_DIM:kv_cols + (hd + 1) * HEAD_DIM]
        s = lax.dot_general(q[:, sl], kh, (((1,), (1,)), ((), ())), preferred_element_type=F32)
        p = jnp.exp2(s - jnp.max(s, axis=-1, keepdims=True))
        l = jnp.sum(p, axis=-1, keepdims=True)
        outs.append((jnp.dot(p.astype(BF16), vh, preferred_element_type=F32) / l).astype(BF16))
    o = jnp.concatenate(outs, axis=1)
    y = jnp.dot(o, wo_ref[...], preferred_element_type=F32)
    o_ref[...] = x + _rms(y, gpost_ref[...])


def memory_xattn(x, gpre, wq, memkv, mem_block, wo, gpost, tm):
    t, d = x.shape
    full = lambda a: pl.BlockSpec(a.shape, lambda i: (0,) * a.ndim)
    kern = functools.partial(_xattn_kernel, q_scale=HEAD_DIM ** -0.5 * LOG2E)
    return pl.pallas_call(
        kern,
        grid=(t // tm,),
        in_specs=[
            pl.BlockSpec((tm, d), lambda i: (i, 0)),
            full(gpre), full(wq),
            pl.BlockSpec((1,) + memkv.shape[1:], lambda i: (mem_block(i), 0, 0)),
            full(wo), full(gpost),
        ],
        out_specs=pl.BlockSpec((tm, d), lambda i: (i, 0)),
        out_shape=jax.ShapeDtypeStruct((t, d), F32),
        compiler_params=_cparams(("parallel",)),
        name="memory_xattn",
    )(x, gpre, wq, memkv, wo, gpost)


def _ffn_kernel(x_ref, gpre_ref, wg_ref, wu_ref, wd_ref, gpost_ref, o_ref, h_ref, acc_ref):
    f = pl.program_id(1)

    @pl.when(f == 0)
    def _():
        h_ref[...] = _rms(x_ref[...], gpre_ref[...]).astype(BF16)
        acc_ref[...] = jnp.zeros_like(acc_ref)

    h = h_ref[...]
    a = jnp.dot(h, wg_ref[...], preferred_element_type=F32)
    b = jnp.dot(h, wu_ref[...], preferred_element_type=F32)
    t = (a * jax.nn.sigmoid(a) * b).astype(BF16)
    acc_ref[...] += jnp.dot(t, wd_ref[...], preferred_element_type=F32)

    @pl.when(f == pl.num_programs(1) - 1)
    def _():
        o_ref[...] = x_ref[...] + _rms(acc_ref[...], gpost_ref[...])


def swiglu_ffn(x, gpre, wg, wu, wd, layer, gpost, tm, tf):
    t, d = x.shape
    ff = wg.shape[2]
    return pl.pallas_call(
        _ffn_kernel,
        grid=(t // tm, ff // tf),
        in_specs=[
            pl.BlockSpec((tm, d), lambda i, f: (i, 0)),
            pl.BlockSpec((1, d), lambda i, f: (0, 0)),
            pl.BlockSpec((None, d, tf), lambda i, f: (layer, 0, f)),
            pl.BlockSpec((None, d, tf), lambda i, f: (layer, 0, f)),
            pl.BlockSpec((None, tf, d), lambda i, f: (layer, f, 0)),
            pl.BlockSpec((1, d), lambda i, f: (0, 0)),
        ],
        out_specs=pl.BlockSpec((tm, d), lambda i, f: (i, 0)),
        out_shape=jax.ShapeDtypeStruct((t, d), F32),
        scratch_shapes=[pltpu.VMEM((tm, d), BF16), pltpu.VMEM((tm, d), F32)],
        compiler_params=_cparams(("parallel", "arbitrary")),
        name="swiglu_ffn",
    )(x, gpre, wg, wu, wd, gpost)


def _rotary_tables(seq):
    pos = jnp.arange(seq)

    def cs(p):
        inv = ROPE_THETA ** (-jnp.arange(0, 64, 2, dtype=F32) / 64)
        ang = p.astype(F32)[:, None] * inv[None, :]
        c, s = jnp.cos(ang), jnp.sin(ang)
        return jnp.concatenate([c, c], axis=-1), jnp.concatenate([-s, s], axis=-1)

    c_row, s_row = cs(pos // GRID_W)
    c_col, s_col = cs(pos % GRID_W)
    c_tok, s_tok = cs(pos)
    return (jnp.concatenate([c_row, c_col], axis=-1), jnp.concatenate([s_row, s_col], axis=-1),
            jnp.concatenate([c_tok, c_tok], axis=-1), jnp.concatenate([s_tok, s_tok], axis=-1))


def _t5_bucket(rel):
    nb = NUM_BUCKETS // 2
    max_exact = nb // 2
    ret = (rel > 0).astype(jnp.int32) * nb
    n = jnp.abs(rel)
    nf = jnp.maximum(n, 1).astype(F32)
    large = max_exact + (jnp.log(nf / max_exact) / math.log(MAX_DISTANCE / max_exact)
                         * (nb - max_exact)).astype(jnp.int32)
    large = jnp.minimum(large, nb - 1)
    return ret + jnp.where(n < max_exact, n, large)


T5_FAR = 91


def _bias_kernel(tab_ref, o_ref, *, tq, tk, r):
    h = pl.program_id(0)
    c_left = tab_ref[h, NUM_BUCKETS // 2 - 1] * LOG2E
    c_right = tab_ref[h, NUM_BUCKETS - 1] * LOG2E
    for t in range(r + 4):
        off = (t - (r + 1)) * tq
        lo = min(max((-off - T5_FAR + 1) // 128 * 128, 0), tk)
        hi = min(max(-((off - tq - T5_FAR + 1) // 128) * 128, 0), tk)
        if lo > 0:
            o_ref[0, t, :, :lo] = jnp.full((tq, lo), c_left, F32)
        if hi < tk:
            o_ref[0, t, :, hi:] = jnp.full((tq, tk - hi), c_right, F32)
        if hi > lo:
            rel = (off + lo + lax.broadcasted_iota(jnp.int32, (tq, hi - lo), 1)
                   - lax.broadcasted_iota(jnp.int32, (tq, hi - lo), 0))
            bucket = _t5_bucket(rel)
            v = jnp.full((tq, hi - lo), tab_ref[h, 0], F32)
            for b in range(1, NUM_BUCKETS):
                v = jnp.where(bucket == b, tab_ref[h, b], v)
            o_ref[0, t, :, lo:hi] = v * LOG2E


def _bias_tiles(rel_bias, tq, tk):
    assert tq >= 128 and tk % tq == 0
    n = np.arange(T5_FAR, 1 << 16, dtype=np.float32)
    far_bucket = np.minimum(8 + (np.log(n / 8) / math.log(MAX_DISTANCE / 8) * 8).astype(np.int32), 15)
    assert NUM_BUCKETS == 32 and far_bucket.min() == 15
    r = tk // tq
    return pl.pallas_call(
        functools.partial(_bias_kernel, tq=tq, tk=tk, r=r),
        grid=(HEADS,),
        in_specs=[pl.BlockSpec(memory_space=pltpu.SMEM)],
        out_specs=pl.BlockSpec((1, r + 4, tq, tk), lambda h: (h, 0, 0, 0)),
        out_shape=jax.ShapeDtypeStruct((HEADS, r + 4, tq, tk), F32),
        compiler_params=_cparams(("parallel",)),
        name="bias_tiles",
    )(rel_bias.T.astype(F32))


def kernel(x_prompt, x_sample, mem_prompt, mem_sample, rel_bias, g_mix_pre, g_mix_post, w_in, lam_q1, lam_k1, lam_q2, lam_k2, g_a_out, g_b_q, g_b_k, g_c_q, g_c_kv, w_c_q_up, w_c_kv_up, w_br_a, w_br_b, w_br_c, w_mix_out, g_x_pre, g_x_post, g_mem, w_x_q, w_x_kv, w_x_out, g_ffn_pre, g_ffn_post, w_ffn_gate, w_ffn_up, w_ffn_down):
    depth = w_in.shape[0]
    d = x_prompt.shape[-1]
    tm_row = 512
    tq_a, tk_a = 256, 1024
    tq, tk, streams = 512, 1024, 1

    groups = [(x.reshape(-1, d), m, x.shape[0], x.shape[1])
              for x, m in ((x_prompt, mem_prompt), (x_sample, mem_sample))]
    tables = _rotary_tables(max(g[3] for g in groups))
    bias = _bias_tiles(rel_bias, tq_a, tk_a)
    row = lambda g: g.reshape(1, -1).astype(F32)
    in_scale = jnp.concatenate([jnp.full((1, Z_AK), A_QK_DIM ** -0.5 * LOG2E, F32),
                                jnp.ones((1, Z_COLS - Z_AK), F32)], axis=-1)
    ones_kv = jnp.ones((1, w_x_kv.shape[-1]), F32)
    kv_group = HEADS // B_KV_HEADS

    w_in_p = w_in_prep(w_in, tr=1024, tn=512)
    w_xkv_all = w_x_kv.astype(BF16)
    w_fg, w_fu, w_fd = w_ffn_gate.astype(BF16), w_ffn_up.astype(BF16), w_ffn_down.astype(BF16)

    xs = [g[0] for g in groups]
    for l in range(depth):
        lam_init = 0.8 - 0.6 * math.exp(-0.3 * l)
        lam = (jnp.exp(jnp.sum(lam_q1[l].astype(F32) * lam_k1[l].astype(F32)))
               - jnp.exp(jnp.sum(lam_q2[l].astype(F32) * lam_k2[l].astype(F32))) + lam_init).reshape(1)

        wq3 = w_c_q_up[l].reshape(C_Q_RANK, HEADS, C_NOPE + C_ROPE)
        w_cq = jnp.concatenate(
            [wq3[..., :C_NOPE].reshape(C_Q_RANK, HEADS * C_NOPE),
             jnp.pad(wq3[..., C_NOPE:], ((0, 0), (0, 0), (0, HEAD_DIM - C_ROPE))).reshape(
                 C_Q_RANK, HEADS * HEAD_DIM)], axis=-1).astype(BF16)
        w_ckv = w_c_kv_up[l].astype(BF16)
        w_a, w_b, w_c = w_br_a[l].astype(BF16), w_br_b[l].astype(BF16), w_br_c[l].astype(BF16)
        w_mix = w_mix_out[l].astype(BF16)
        w_xq, w_xo = w_x_q[l].astype(BF16), w_x_out[l].astype(BF16)

        for gi, (_, mem, nseq, seq) in enumerate(groups):
            x = xs[gi]
            blocks_per_seq = seq // tm_row
            z = rms_matmul(x, row(g_mix_pre[l]), w_in_p, l, in_scale, tm=1024, tn=512)
            qb, kb, cq, ckv, kr = mixer_prep(z, tables, lambda i: i % blocks_per_seq, row(g_b_q[l]), row(g_b_k[l]),
                                             row(g_c_q[l]), row(g_c_kv[l]), w_cq, w_ckv, tm_row)
            oa = attention_a(lam, z, bias, row(g_a_out[l]), nseq, seq, tq_a, tk_a, 1.0 - lam_init)
            ob = attention([(qb, lambda h: h)],
                           [(kb, lambda h: h // kv_group)],
                           (z, lambda h: Z_BV // HEAD_DIM + h // kv_group),
                           nseq, seq, tq, tk, streams)
            oc = attention([(cq, lambda h: h), (cq, lambda h: HEADS + h)],
                           [(ckv, lambda h: 2 * h), (kr, lambda h: 0)],
                           (ckv, lambda h: 2 * h + 1),
                           nseq, seq, tq, tk, streams)
            merged = gated_merge(oa, ob, oc, w_a, w_b, w_c, z, tm=1024, tn=512)
            x = mm_postnorm_residual(merged, w_mix, row(g_mix_post[l]), x, tm=tm_row)

            mem2 = mem.reshape(nseq * MEM_TOKENS, d)
            memkv = rms_matmul(mem2, row(g_mem[l]), w_xkv_all, l, ones_kv, tm=mem2.shape[0], tn=512)
            memkv = memkv.reshape(nseq, MEM_TOKENS, -1)
            x = memory_xattn(x, row(g_x_pre[l]), w_xq, memkv, lambda i: i // blocks_per_seq, w_xo,
                             row(g_x_post[l]), tm_row)
            xs[gi] = swiglu_ffn(x, row(g_ffn_pre[l]), w_fg, w_fu, w_fd, l, row(g_ffn_post[l]), tm=tm_row, tf=512)

    return tuple(x.reshape(g[2], g[3], d) for x, g in zip(xs, groups))
```

```python
import functools
import math

import jax
import jax.numpy as jnp
import numpy as np
from jax import lax
from jax.experimental import pallas as pl
from jax.experimental.pallas import tpu as pltpu

F32 = jnp.float32
BF16 = jnp.bfloat16

EPS = 1e-6
ROPE_THETA = 10000.0
LOG2E = 1.4426950408889634
GRID_W = 64
NUM_BUCKETS = 32
MAX_DISTANCE = 128

D_MODEL = 2048
HEADS = 8
HEAD_DIM = 128
A_QK_DIM = 64
B_KV_HEADS = 2
C_Q_RANK = 512
C_KV_RANK = 256
C_NOPE = 128
C_ROPE = 64
X_HEADS = 4
MEM_TOKENS = 256

Z_AQ, Z_AK, Z_AV, Z_BQ, Z_BK, Z_BV, Z_CQA, Z_CKVA, Z_CKR = 0, 1024, 2048, 3072, 4096, 4352, 4608, 5120, 5376
Z_RAW_SPLIT = 5440
Z_GATES = 5632
Z_COLS = Z_GATES + 3 * D_MODEL

VMEM_LIMIT = 56 * 1024 * 1024


def _cparams(sem):
    return pltpu.CompilerParams(dimension_semantics=sem, vmem_limit_bytes=VMEM_LIMIT)


def _rms(x, g):
    return x * lax.rsqrt(jnp.mean(x * x, axis=-1, keepdims=True) + EPS) * g


def _swap32(x):
    lane = lax.broadcasted_iota(jnp.int32, x.shape, 1)
    return jnp.where((lane % 64) < 32, pltpu.roll(x, 96, 1), pltpu.roll(x, 32, 1))


def _rope(x, cos, sin_signed):
    return x * cos + _swap32(x) * sin_signed


def _rms_matmul_kernel(x_ref, g_ref, w_ref, cs_ref, o_ref, xn_ref):
    @pl.when(pl.program_id(1) == 0)
    def _():
        xn_ref[...] = _rms(x_ref[...].astype(F32), g_ref[...]).astype(BF16)

    acc = jnp.dot(xn_ref[...], w_ref[...], preferred_element_type=F32)
    o_ref[...] = (acc * cs_ref[...]).astype(o_ref.dtype)


def rms_matmul(x, g, w, layer, colscale, tm, tn, out_dtype=BF16):
    m, k = x.shape
    n = w.shape[2]
    return pl.pallas_call(
        _rms_matmul_kernel,
        grid=(m // tm, n // tn),
        in_specs=[
            pl.BlockSpec((tm, k), lambda i, j: (i, 0)),
            pl.BlockSpec((1, k), lambda i, j: (0, 0)),
            pl.BlockSpec((None, k, tn), lambda i, j: (layer, 0, j)),
            pl.BlockSpec((1, tn), lambda i, j: (0, j)),
        ],
        out_specs=pl.BlockSpec((tm, tn), lambda i, j: (i, j)),
        out_shape=jax.ShapeDtypeStruct((m, n), out_dtype),
        scratch_shapes=[pltpu.VMEM((tm, k), BF16)],
        compiler_params=_cparams(("parallel", "arbitrary")),
        name="rms_matmul",
    )(x, g, w, colscale)


def _w_in_prep_kernel(lo_ref, hi_ref, o_ref, *, split_block, split_lane):
    j = pl.program_id(2)

    @pl.when(j < split_block)
    def _():
        o_ref[...] = hi_ref[...].astype(BF16)

    @pl.when(j == split_block)
    def _():
        lane = lax.broadcasted_iota(jnp.int32, hi_ref.shape, 1)
        o_ref[...] = jnp.where(lane < split_lane, hi_ref[...], 0.0).astype(BF16)

    @pl.when(j > split_block)
    def _():
        o_ref[...] = jnp.concatenate([lo_ref[:, split_lane:], hi_ref[:, :split_lane]], axis=1).astype(BF16)


def w_in_prep(w_in, tr, tn):
    depth, d, raw = w_in.shape
    pad = Z_GATES - Z_RAW_SPLIT
    assert Z_COLS == raw + pad and pad < tn
    split_block, split_lane = divmod(Z_RAW_SPLIT, tn)
    last = (raw - 1) // tn
    return pl.pallas_call(
        functools.partial(_w_in_prep_kernel, split_block=split_block, split_lane=split_lane),
        grid=(depth, d // tr, Z_COLS // tn),
        in_specs=[
            pl.BlockSpec((None, tr, tn), lambda l, i, j: (l, i, jnp.maximum(j - 1, 0))),
            pl.BlockSpec((None, tr, tn), lambda l, i, j: (l, i, jnp.minimum(j, last))),
        ],
        out_specs=pl.BlockSpec((None, tr, tn), lambda l, i, j: (l, i, j)),
        out_shape=jax.ShapeDtypeStruct((depth, d, Z_COLS), BF16),
        compiler_params=_cparams(("parallel", "parallel", "arbitrary")),
        name="w_in_prep",
    )(w_in, w_in)


def _prep_kernel(bq_ref, bkv_ref, cqa_ref, ckk_ref, cos_a_ref, sin_a_ref, cos_r_ref, sin_r_ref,
                 gbq_ref, gbk_ref, gcq_ref, gckv_ref, wq_ref, wkv_ref,
                 qb_out, kb_out, cq_out, ckv_out, kr_out, *, b_scale, c_scale):
    cos_a, sin_a = cos_a_ref[...], sin_a_ref[...]
    cos_r, sin_r = cos_r_ref[...], sin_r_ref[...]

    for h in range(HEADS):
        sl = slice(h * HEAD_DIM, (h + 1) * HEAD_DIM)
        y = _rms(bq_ref[:, sl].astype(F32), gbq_ref[...])
        qb_out[:, sl] = (_rope(y, cos_a, sin_a) * b_scale).astype(BF16)
    for n in range(B_KV_HEADS):
        sl = slice(n * HEAD_DIM, (n + 1) * HEAD_DIM)
        y = _rms(bkv_ref[:, sl].astype(F32), gbk_ref[...])
        kb_out[:, sl] = _rope(y, cos_a, sin_a).astype(BF16)

    xq = _rms(cqa_ref[...].astype(F32), gcq_ref[...]).astype(BF16)
    cq = jnp.dot(xq, wq_ref[...], preferred_element_type=F32)
    nope_cols = HEADS * C_NOPE
    cq_out[:, :nope_cols] = (cq[:, :nope_cols] * c_scale).astype(BF16)
    for h in range(HEADS):
        sl = slice(nope_cols + h * HEAD_DIM, nope_cols + (h + 1) * HEAD_DIM)
        cq_out[:, sl] = (_rope(cq[:, sl], cos_r, sin_r) * c_scale).astype(BF16)

    xkv = _rms(ckk_ref[:, :C_KV_RANK].astype(F32), gckv_ref[...]).astype(BF16)
    ckv_out[...] = jnp.dot(xkv, wkv_ref[...], preferred_element_type=F32).astype(BF16)
    kr = ckk_ref[:, C_KV_RANK:C_KV_RANK + HEAD_DIM].astype(F32)
    kr_out[...] = _rope(kr, cos_r, sin_r).astype(BF16)


def mixer_prep(z, tables, pos_block, gbq, gbk, gcq, gckv, wq, wkv, tm):
    t = z.shape[0]
    cos_a, sin_a, cos_r, sin_r = tables
    row = lambda w: pl.BlockSpec((tm, w), lambda i: (i, 0))
    tab = pl.BlockSpec((tm, HEAD_DIM), lambda i: (pos_block(i), 0))
    full = lambda a: pl.BlockSpec(a.shape, lambda i: (0,) * a.ndim)
    kern = functools.partial(_prep_kernel, b_scale=HEAD_DIM ** -0.5 * LOG2E,
                             c_scale=(C_NOPE + C_ROPE) ** -0.5 * LOG2E)
    return pl.pallas_call(
        kern,
        grid=(t // tm,),
        in_specs=[
            pl.BlockSpec((tm, 1024), lambda i: (i, Z_BQ // 1024)),
            pl.BlockSpec((tm, 512), lambda i: (i, Z_BK // 512)),
            pl.BlockSpec((tm, 512), lambda i: (i, Z_CQA // 512)),
            pl.BlockSpec((tm, 512), lambda i: (i, Z_CKVA // 512)),
            tab, tab, tab, tab,
            full(gbq), full(gbk), full(gcq), full(gckv), full(wq), full(wkv),
        ],
        out_specs=[row(1024), row(256), row(2048), row(2048), row(HEAD_DIM)],
        out_shape=[
            jax.ShapeDtypeStruct((t, 1024), BF16),
            jax.ShapeDtypeStruct((t, 256), BF16),
            jax.ShapeDtypeStruct((t, 2048), BF16),
            jax.ShapeDtypeStruct((t, 2048), BF16),
            jax.ShapeDtypeStruct((t, HEAD_DIM), BF16),
        ],
        compiler_params=_cparams(("parallel",)),
        name="mixer_prep",
    )(z, z, z, z, cos_a, sin_a, cos_r, sin_r, gbq, gbk, gcq, gckv, wq, wkv)


def _flash(q, k_refs, v_ref, s_refs, nk, tk, chunk_fn=None, tile_bias_fn=None, const_bias_fn=None):
    m_rows = q.shape[0]
    dv = v_ref.shape[1]
    ones = jnp.ones((tk, dv), BF16)

    def rows_of(ref, chunk):
        if isinstance(chunk, int):
            return ref[chunk * tk:(chunk + 1) * tk, :]
        return ref[pl.ds(pl.multiple_of(chunk * tk, tk), tk), :]

    def scores(chunk):
        ks = [rows_of(r, chunk) for r in k_refs]
        k = ks[0] if len(ks) == 1 else jnp.concatenate(ks, axis=1)
        return lax.dot_general(q, k, (((1,), (1,)), ((), ())), preferred_element_type=F32)

    def update(pos, chunk, s, carry):
        m, acc = carry
        sb = None if tile_bias_fn is None else tile_bias_fn(pos, chunk, s)
        if sb is not None:
            s = sb
            m_new = jnp.maximum(m, jnp.max(s, axis=-1, keepdims=True))
            shift = m_new
        elif const_bias_fn is not None:
            c = const_bias_fn(pos, chunk)
            m_new = jnp.maximum(m, jnp.max(s, axis=-1, keepdims=True) + c)
            shift = m_new - c
        else:
            m_new = jnp.maximum(m, jnp.max(s, axis=-1, keepdims=True))
            shift = m_new
        p = jnp.exp2((s - shift).astype(BF16))
        alpha = jnp.exp2(m - m_new)
        v_aug = jnp.concatenate([rows_of(v_ref, chunk), ones], axis=1)
        pv = jnp.dot(p, v_aug, preferred_element_type=F32)
        return m_new, alpha * acc + pv

    chunks = [pos if chunk_fn is None else chunk_fn(pos) for pos in range(nk)]
    carry = (jnp.full((m_rows, 1), -jnp.inf, F32), jnp.zeros((m_rows, 2 * dv), F32))
    s_refs[0][...] = scores(chunks[0])
    for pos in range(nk):
        if pos + 1 < nk:
            s_refs[(pos + 1) % 2][...] = scores(chunks[pos + 1])
        carry = update(pos, chunks[pos], s_refs[pos % 2][...], carry)
    m, acc = carry
    return m, acc[:, dv:], acc[:, :dv]


def _attn_kernel(*refs, n_q, n_k, nk, tk, streams):
    n_s = 2 * streams
    q_refs, k_refs = refs[:n_q], refs[n_q:n_q + n_k]
    v_ref, o_ref, s_refs = refs[n_q + n_k], refs[-n_s - 1], refs[-n_s:]
    rows = o_ref.shape[0] // streams
    for t in range(streams):
        sl = slice(t * rows, (t + 1) * rows)
        qs = [r[sl, :] for r in q_refs]
        q = qs[0] if n_q == 1 else jnp.concatenate(qs, axis=1)
        _, l, acc = _flash(q, k_refs, v_ref, s_refs[2 * t:2 * t + 2], nk, tk)
        o_ref[sl, :] = (acc / l).astype(o_ref.dtype)


def _attn_a_kernel(lam_ref, far_ref, q_ref, k_ref, v_ref, bias_ref, g_ref, *rest, tq, tk, nk, r, post_scale,
                   streams):
    n_s = 2 * streams
    o_ref, s_refs = rest[-n_s - 1], rest[-n_s:]
    h = pl.program_id(1)
    near = (0, 1, nk - 1)

    for t in range(streams):
        qi = pl.program_id(2) * streams + t
        diag = qi // r
        q = q_ref[t * tq:(t + 1) * tq, :]
        lane = lax.broadcasted_iota(jnp.int32, q.shape, 1)
        zero = jnp.zeros_like(q)
        q2 = jnp.concatenate([jnp.where(lane < A_QK_DIM, q, zero), jnp.where(lane >= A_QK_DIM, q, zero)], axis=0)

        def chunk_fn(pos, diag=diag):
            return (diag + pos) % nk

        def tile_bias_fn(pos, chunk, s, qi=qi):
            if pos not in near:
                return None
            b = bias_ref[0, jnp.clip(chunk * r - qi, -r - 1, 2) + r + 1]
            return jnp.concatenate([s[:tq] + b, s[tq:] + b], axis=0)

        def const_bias_fn(pos, chunk, diag=diag):
            return jnp.where(chunk > diag, far_ref[h, 1], far_ref[h, 0])

        _, l, acc = _flash(q2, [k_ref], v_ref, s_refs[2 * t:2 * t + 2], nk, tk, chunk_fn, tile_bias_fn,
                           const_bias_fn)
        o = acc / l
        d = o[:tq] - lam_ref[0] * o[tq:]
        o_ref[t * tq:(t + 1) * tq, :] = (_rms(d, g_ref[...]) * post_scale).astype(o_ref.dtype)


def attention(q_parts, k_parts, v_part, nseq, seq, tq, tk, streams):
    rows = tq * streams
    nq = seq // rows
    in_specs, args = [], []
    for a, cf in q_parts:
        in_specs.append(pl.BlockSpec((rows, HEAD_DIM), lambda s, h, i, cf=cf: (s * nq + i, cf(h))))
        args.append(a)
    for a, cf in list(k_parts) + [v_part]:
        in_specs.append(pl.BlockSpec((seq, HEAD_DIM), lambda s, h, i, cf=cf: (s, cf(h))))
        args.append(a)
    kern = functools.partial(_attn_kernel, n_q=len(q_parts), n_k=len(k_parts), nk=seq // tk, tk=tk,
                             streams=streams)
    return pl.pallas_call(
        kern,
        grid=(nseq, HEADS, nq),
        in_specs=in_specs,
        out_specs=pl.BlockSpec((rows, HEAD_DIM), lambda s, h, i: (s * nq + i, h)),
        out_shape=jax.ShapeDtypeStruct((nseq * seq, HEADS * HEAD_DIM), BF16),
        scratch_shapes=[pltpu.VMEM((tq, tk), F32)] * (2 * streams),
        compiler_params=_cparams(("parallel", "parallel", "arbitrary")),
        name="attention",
    )(*args)


def attention_a(lam, z, bias, g, nseq, seq, tq, tk, post_scale, streams):
    rows = tq * streams
    nq = seq // rows
    r = tk // tq
    far = bias[:, :: r + 3, 0, 0]
    in_specs = [
        pl.BlockSpec(memory_space=pltpu.SMEM),
        pl.BlockSpec(memory_space=pltpu.SMEM),
        pl.BlockSpec((rows, HEAD_DIM), lambda s, h, i: (s * nq + i, Z_AQ // HEAD_DIM + h)),
        pl.BlockSpec((seq, HEAD_DIM), lambda s, h, i: (s, Z_AK // HEAD_DIM + h)),
        pl.BlockSpec((seq, HEAD_DIM), lambda s, h, i: (s, Z_AV // HEAD_DIM + h)),
        pl.BlockSpec((1, r + 4, tq, tk), lambda s, h, i: (h, 0, 0, 0)),
        pl.BlockSpec((1, HEAD_DIM), lambda s, h, i: (0, 0)),
    ]
    kern = functools.partial(_attn_a_kernel, tq=tq, tk=tk, nk=seq // tk, r=r, post_scale=post_scale,
                             streams=streams)
    return pl.pallas_call(
        kern,
        grid=(nseq, HEADS, nq),
        in_specs=in_specs,
        out_specs=pl.BlockSpec((rows, HEAD_DIM), lambda s, h, i: (s * nq + i, h)),
        out_shape=jax.ShapeDtypeStruct((nseq * seq, HEADS * HEAD_DIM), BF16),
        scratch_shapes=[pltpu.VMEM((2 * tq, tk), F32)] * (2 * streams),
        compiler_params=_cparams(("parallel", "parallel", "arbitrary")),
        name="attention_a",
    )(lam, far, z, z, z, bias, g)


def _merge_kernel(oa_ref, ob_ref, oc_ref, wa_ref, wb_ref, wc_ref, ga_ref, gb_ref, gc_ref, o_ref):
    def branch(o_r, w_r, g_r):
        y = jnp.dot(o_r[...], w_r[...], preferred_element_type=F32)
        return jax.nn.sigmoid(g_r[...].astype(F32)) * y

    merged = branch(oa_ref, wa_ref, ga_ref) + branch(ob_ref, wb_ref, gb_ref) + branch(oc_ref, wc_ref, gc_ref)
    o_ref[...] = merged.astype(o_ref.dtype)


def gated_merge(oa, ob, oc, wa, wb, wc, z, tm, tn):
    t, k = oa.shape
    n = wa.shape[1]
    o_spec = pl.BlockSpec((tm, k), lambda i, j: (i, 0))
    w_spec = pl.BlockSpec((k, tn), lambda i, j: (0, j))
    gate = lambda b: pl.BlockSpec((tm, tn), lambda i, j: (i, (Z_GATES + b * n) // tn + j))
    return pl.pallas_call(
        _merge_kernel,
        grid=(t // tm, n // tn),
        in_specs=[o_spec, o_spec, o_spec, w_spec, w_spec, w_spec, gate(0), gate(1), gate(2)],
        out_specs=pl.BlockSpec((tm, tn), lambda i, j: (i, j)),
        out_shape=jax.ShapeDtypeStruct((t, n), BF16),
        compiler_params=_cparams(("parallel", "arbitrary")),
        name="gated_merge",
    )(oa, ob, oc, wa, wb, wc, z, z, z)


def _mm_postnorm_kernel(a_ref, w_ref, g_ref, x_ref, o_ref):
    y = jnp.dot(a_ref[...], w_ref[...], preferred_element_type=F32)
    o_ref[...] = x_ref[...] + _rms(y, g_ref[...])


def mm_postnorm_residual(a, w, g, x, tm):
    t, k = a.shape
    n = w.shape[1]
    return pl.pallas_call(
        _mm_postnorm_kernel,
        grid=(t // tm,),
        in_specs=[
            pl.BlockSpec((tm, k), lambda i: (i, 0)),
            pl.BlockSpec((k, n), lambda i: (0, 0)),
            pl.BlockSpec((1, n), lambda i: (0, 0)),
            pl.BlockSpec((tm, n), lambda i: (i, 0)),
        ],
        out_specs=pl.BlockSpec((tm, n), lambda i: (i, 0)),
        out_shape=jax.ShapeDtypeStruct((t, n), F32),
        compiler_params=_cparams(("parallel",)),
        name="mm_postnorm_residual",
    )(a, w, g, x)


def _xattn_kernel(x_ref, gpre_ref, wq_ref, kv_ref, wo_ref, gpost_ref, o_ref, *, q_scale):
    x = x_ref[...]
    h = _rms(x, gpre_ref[...]).astype(BF16)
    q = (jnp.dot(h, wq_ref[...], preferred_element_type=F32) * q_scale).astype(BF16)
    kv_cols = X_HEADS * HEAD_DIM
    outs = []
    for hd in range(X_HEADS):
        sl = slice(hd * HEAD_DIM, (hd + 1) * HEAD_DIM)
        kh = kv_ref[0, :, sl]
        vh = kv_ref[0, :, kv_cols + hd * HEAD_DIM:kv_cols + (hd + 1) * HEAD_DIM]
        s = lax.dot_general(q[:, sl], kh, (((1,), (1,)), ((), ())), preferred_element_type=F32)
        p = jnp.exp2(s - jnp.max(s, axis=-1, keepdims=True))
        l = jnp.sum(p, axis=-1, keepdims=True)
        outs.append((jnp.dot(p.astype(BF16), vh, preferred_element_type=F32) / l).astype(BF16))
    o = jnp.concatenate(outs, axis=1)
    y = jnp.dot(o, wo_ref[...], preferred_element_type=F32)
    o_ref[...] = x + _rms(y, gpost_ref[...])


def memory_xattn(x, gpre, wq, memkv, mem_block, wo, gpost, tm):
    t, d = x.shape
    full = lambda a: pl.BlockSpec(a.shape, lambda i: (0,) * a.ndim)
    kern = functools.partial(_xattn_kernel, q_scale=HEAD_DIM ** -0.5 * LOG2E)
    return pl.pallas_call(
        kern,
        grid=(t // tm,),
        in_specs=[
            pl.BlockSpec((tm, d), lambda i: (i, 0)),
            full(gpre), full(wq),
            pl.BlockSpec((1,) + memkv.shape[1:], lambda i: (mem_block(i), 0, 0)),
            full(wo), full(gpost),
        ],
        out_specs=pl.BlockSpec((tm, d), lambda i: (i, 0)),
        out_shape=jax.ShapeDtypeStruct((t, d), F32),
        compiler_params=_cparams(("parallel",)),
        name="memory_xattn",
    )(x, gpre, wq, memkv, wo, gpost)


def _ffn_kernel(x_ref, gpre_ref, wg_ref, wu_ref, wd_ref, gpost_ref, o_ref, h_ref, acc_ref):
    f = pl.program_id(1)

    @pl.when(f == 0)
    def _():
        h_ref[...] = _rms(x_ref[...], gpre_ref[...]).astype(BF16)
        acc_ref[...] = jnp.zeros_like(acc_ref)

    h = h_ref[...]
    a = jnp.dot(h, wg_ref[...], preferred_element_type=F32)
    b = jnp.dot(h, wu_ref[...], preferred_element_type=F32)
    t = (a * jax.nn.sigmoid(a) * b).astype(BF16)
    acc_ref[...] += jnp.dot(t, wd_ref[...], preferred_element_type=F32)

    @pl.when(f == pl.num_programs(1) - 1)
    def _():
        o_ref[...] = x_ref[...] + _rms(acc_ref[...], gpost_ref[...])


def swiglu_ffn(x, gpre, wg, wu, wd, layer, gpost, tm, tf):
    t, d = x.shape
    ff = wg.shape[2]
    return pl.pallas_call(
        _ffn_kernel,
        grid=(t // tm, ff // tf),
        in_specs=[
            pl.BlockSpec((tm, d), lambda i, f: (i, 0)),
            pl.BlockSpec((1, d), lambda i, f: (0, 0)),
            pl.BlockSpec((None, d, tf), lambda i, f: (layer, 0, f)),
            pl.BlockSpec((None, d, tf), lambda i, f: (layer, 0, f)),
            pl.BlockSpec((None, tf, d), lambda i, f: (layer, f, 0)),
            pl.BlockSpec((1, d), lambda i, f: (0, 0)),
        ],
        out_specs=pl.BlockSpec((tm, d), lambda i, f: (i, 0)),
        out_shape=jax.ShapeDtypeStruct((t, d), F32),
        scratch_shapes=[pltpu.VMEM((tm, d), BF16), pltpu.VMEM((tm, d), F32)],
        compiler_params=_cparams(("parallel", "arbitrary")),
        name="swiglu_ffn",
    )(x, gpre, wg, wu, wd, gpost)


def _rotary_tables(seq):
    pos = jnp.arange(seq)

    def cs(p):
        inv = ROPE_THETA ** (-jnp.arange(0, 64, 2, dtype=F32) / 64)
        ang = p.astype(F32)[:, None] * inv[None, :]
        c, s = jnp.cos(ang), jnp.sin(ang)
        return jnp.concatenate([c, c], axis=-1), jnp.concatenate([-s, s], axis=-1)

    c_row, s_row = cs(pos // GRID_W)
    c_col, s_col = cs(pos % GRID_W)
    c_tok, s_tok = cs(pos)
    return (jnp.concatenate([c_row, c_col], axis=-1), jnp.concatenate([s_row, s_col], axis=-1),
            jnp.concatenate([c_tok, c_tok], axis=-1), jnp.concatenate([s_tok, s_tok], axis=-1))


def _t5_bucket(rel):
    nb = NUM_BUCKETS // 2
    max_exact = nb // 2
    ret = (rel > 0).astype(jnp.int32) * nb
    n = jnp.abs(rel)
    nf = jnp.maximum(n, 1).astype(F32)
    large = max_exact + (jnp.log(nf / max_exact) / math.log(MAX_DISTANCE / max_exact)
                         * (nb - max_exact)).astype(jnp.int32)
    large = jnp.minimum(large, nb - 1)
    return ret + jnp.where(n < max_exact, n, large)


T5_FAR = 91


def _bias_kernel(tab_ref, o_ref, *, tq, tk, r):
    h = pl.program_id(0)
    c_left = tab_ref[h, NUM_BUCKETS // 2 - 1] * LOG2E
    c_right = tab_ref[h, NUM_BUCKETS - 1] * LOG2E
    for t in range(r + 4):
        off = (t - (r + 1)) * tq
        lo = min(max((-off - T5_FAR + 1) // 128 * 128, 0), tk)
        hi = min(max(-((off - tq - T5_FAR + 1) // 128) * 128, 0), tk)
        if lo > 0:
            o_ref[0, t, :, :lo] = jnp.full((tq, lo), c_left, F32)
        if hi < tk:
            o_ref[0, t, :, hi:] = jnp.full((tq, tk - hi), c_right, F32)
        if hi > lo:
            rel = (off + lo + lax.broadcasted_iota(jnp.int32, (tq, hi - lo), 1)
                   - lax.broadcasted_iota(jnp.int32, (tq, hi - lo), 0))
            bucket = _t5_bucket(rel)
            v = jnp.full((tq, hi - lo), tab_ref[h, 0], F32)
            for b in range(1, NUM_BUCKETS):
                v = jnp.where(bucket == b, tab_ref[h, b], v)
            o_ref[0, t, :, lo:hi] = v * LOG2E


def _bias_tiles(rel_bias, tq, tk):
    assert tq >= 128 and tk % tq == 0
    n = np.arange(T5_FAR, 1 << 16, dtype=np.float32)
    far_bucket = np.minimum(8 + (np.log(n / 8) / math.log(MAX_DISTANCE / 8) * 8).astype(np.int32), 15)
    assert NUM_BUCKETS == 32 and far_bucket.min() == 15
    r = tk // tq
    return pl.pallas_call(
        functools.partial(_bias_kernel, tq=tq, tk=tk, r=r),
        grid=(HEADS,),
        in_specs=[pl.BlockSpec(memory_space=pltpu.SMEM)],
        out_specs=pl.BlockSpec((1, r + 4, tq, tk), lambda h: (h, 0, 0, 0)),
        out_shape=jax.ShapeDtypeStruct((HEADS, r + 4, tq, tk), F32),
        compiler_params=_cparams(("parallel",)),
        name="bias_tiles",
    )(rel_bias.T.astype(F32))


def kernel(x_prompt, x_sample, mem_prompt, mem_sample, rel_bias, g_mix_pre, g_mix_post, w_in, lam_q1, lam_k1, lam_q2, lam_k2, g_a_out, g_b_q, g_b_k, g_c_q, g_c_kv, w_c_q_up, w_c_kv_up, w_br_a, w_br_b, w_br_c, w_mix_out, g_x_pre, g_x_post, g_mem, w_x_q, w_x_kv, w_x_out, g_ffn_pre, g_ffn_post, w_ffn_gate, w_ffn_up, w_ffn_down):
    depth = w_in.shape[0]
    d = x_prompt.shape[-1]
    tm_row = 512
    tq_a, tk_a = 256, 1024
    tq, tk = 512, 1024

    groups = [(x.reshape(-1, d), m, x.shape[0], x.shape[1])
              for x, m in ((x_prompt, mem_prompt), (x_sample, mem_sample))]
    tables = _rotary_tables(max(g[3] for g in groups))
    bias = _bias_tiles(rel_bias, tq_a, tk_a)
    row = lambda g: g.reshape(1, -1).astype(F32)
    in_scale = jnp.concatenate([jnp.full((1, Z_AK), A_QK_DIM ** -0.5 * LOG2E, F32),
                                jnp.ones((1, Z_COLS - Z_AK), F32)], axis=-1)
    ones_kv = jnp.ones((1, w_x_kv.shape[-1]), F32)
    kv_group = HEADS // B_KV_HEADS

    w_in_p = w_in_prep(w_in, tr=1024, tn=512)
    w_xkv_all = w_x_kv.astype(BF16)
    w_fg, w_fu, w_fd = w_ffn_gate.astype(BF16), w_ffn_up.astype(BF16), w_ffn_down.astype(BF16)

    xs = [g[0] for g in groups]
    for l in range(depth):
        lam_init = 0.8 - 0.6 * math.exp(-0.3 * l)
        lam = (jnp.exp(jnp.sum(lam_q1[l].astype(F32) * lam_k1[l].astype(F32)))
               - jnp.exp(jnp.sum(lam_q2[l].astype(F32) * lam_k2[l].astype(F32))) + lam_init).reshape(1)

        wq3 = w_c_q_up[l].reshape(C_Q_RANK, HEADS, C_NOPE + C_ROPE)
        w_cq = jnp.concatenate(
            [wq3[..., :C_NOPE].reshape(C_Q_RANK, HEADS * C_NOPE),
             jnp.pad(wq3[..., C_NOPE:], ((0, 0), (0, 0), (0, HEAD_DIM - C_ROPE))).reshape(
                 C_Q_RANK, HEADS * HEAD_DIM)], axis=-1).astype(BF16)
        w_ckv = w_c_kv_up[l].astype(BF16)
        w_a, w_b, w_c = w_br_a[l].astype(BF16), w_br_b[l].astype(BF16), w_br_c[l].astype(BF16)
        w_mix = w_mix_out[l].astype(BF16)
        w_xq, w_xo = w_x_q[l].astype(BF16), w_x_out[l].astype(BF16)

        for gi, (_, mem, nseq, seq) in enumerate(groups):
            x = xs[gi]
            blocks_per_seq = seq // tm_row
            streams = 2 if seq // tk <= 4 else 1
            z = rms_matmul(x, row(g_mix_pre[l]), w_in_p, l, in_scale, tm=1024, tn=512)
            qb, kb, cq, ckv, kr = mixer_prep(z, tables, lambda i: i % blocks_per_seq, row(g_b_q[l]), row(g_b_k[l]),
                                             row(g_c_q[l]), row(g_c_kv[l]), w_cq, w_ckv, tm_row)
            oa = attention_a(lam, z, bias, row(g_a_out[l]), nseq, seq, tq_a, tk_a, 1.0 - lam_init, streams)
            ob = attention([(qb, lambda h: h)],
                           [(kb, lambda h: h // kv_group)],
                           (z, lambda h: Z_BV // HEAD_DIM + h // kv_group),
                           nseq, seq, tq, tk, streams)
            oc = attention([(cq, lambda h: h), (cq, lambda h: HEADS + h)],
                           [(ckv, lambda h: 2 * h), (kr, lambda h: 0)],
                           (ckv, lambda h: 2 * h + 1),
                           nseq, seq, tq, tk, streams)
            merged = gated_merge(oa, ob, oc, w_a, w_b, w_c, z, tm=1024, tn=512)
            x = mm_postnorm_residual(merged, w_mix, row(g_mix_post[l]), x, tm=tm_row)

            mem2 = mem.reshape(nseq * MEM_TOKENS, d)
            memkv = rms_matmul(mem2, row(g_mem[l]), w_xkv_all, l, ones_kv, tm=mem2.shape[0], tn=512)
            memkv = memkv.reshape(nseq, MEM_TOKENS, -1)
            x = memory_xattn(x, row(g_x_pre[l]), w_xq, memkv, lambda i: i // blocks_per_seq, w_xo,
                             row(g_x_post[l]), tm_row)
            xs[gi] = swiglu_ffn(x, row(g_ffn_pre[l]), w_fg, w_fu, w_fd, l, row(g_ffn_post[l]), tm=tm_row, tf=512)

    return tuple(x.reshape(g[2], g[3], d) for x, g in zip(xs, groups))
```

```python
import functools
import math

import jax
import jax.numpy as jnp
import numpy as np
from jax import lax
from jax.experimental import pallas as pl
from jax.experimental.pallas import tpu as pltpu

F32 = jnp.float32
BF16 = jnp.bfloat16

EPS = 1e-6
ROPE_THETA = 10000.0
LOG2E = 1.4426950408889634
GRID_W = 64
NUM_BUCKETS = 32
MAX_DISTANCE = 128

D_MODEL = 2048
HEADS = 8
HEAD_DIM = 128
A_QK_DIM = 64
B_KV_HEADS = 2
C_Q_RANK = 512
C_KV_RANK = 256
C_NOPE = 128
C_ROPE = 64
X_HEADS = 4
MEM_TOKENS = 256

Z_AQ, Z_AK, Z_AV, Z_BQ, Z_BK, Z_BV, Z_CQA, Z_CKVA, Z_CKR = 0, 1024, 2048, 3072, 4096, 4352, 4608, 5120, 5376
Z_RAW_SPLIT = 5440
Z_GATES = 5632
Z_COLS = Z_GATES + 3 * D_MODEL

VMEM_LIMIT = 56 * 1024 * 1024


def _cparams(sem):
    return pltpu.CompilerParams(dimension_semantics=sem, vmem_limit_bytes=VMEM_LIMIT)


def _rms(x, g):
    return x * lax.rsqrt(jnp.mean(x * x, axis=-1, keepdims=True) + EPS) * g


def _swap32(x):
    lane = lax.broadcasted_iota(jnp.int32, x.shape, 1)
    return jnp.where((lane % 64) < 32, pltpu.roll(x, 96, 1), pltpu.roll(x, 32, 1))


def _rope(x, cos, sin_signed):
    return x * cos + _swap32(x) * sin_signed


def _rms_matmul_kernel(x_ref, g_ref, w_ref, cs_ref, o_ref, xn_ref):
    @pl.when(pl.program_id(1) == 0)
    def _():
        xn_ref[...] = _rms(x_ref[...].astype(F32), g_ref[...]).astype(BF16)

    acc = jnp.dot(xn_ref[...], w_ref[...], preferred_element_type=F32)
    o_ref[...] = (acc * cs_ref[...]).astype(o_ref.dtype)


def rms_matmul(x, g, w, layer, colscale, tm, tn, out_dtype=BF16):
    m, k = x.shape
    n = w.shape[2]
    return pl.pallas_call(
        _rms_matmul_kernel,
        grid=(m // tm, n // tn),
        in_specs=[
            pl.BlockSpec((tm, k), lambda i, j: (i, 0)),
            pl.BlockSpec((1, k), lambda i, j: (0, 0)),
            pl.BlockSpec((None, k, tn), lambda i, j: (layer, 0, j)),
            pl.BlockSpec((1, tn), lambda i, j: (0, j)),
        ],
        out_specs=pl.BlockSpec((tm, tn), lambda i, j: (i, j)),
        out_shape=jax.ShapeDtypeStruct((m, n), out_dtype),
        scratch_shapes=[pltpu.VMEM((tm, k), BF16)],
        compiler_params=_cparams(("parallel", "arbitrary")),
        name="rms_matmul",
    )(x, g, w, colscale)


def _w_in_prep_kernel(lo_ref, hi_ref, o_ref, *, split_block, split_lane):
    j = pl.program_id(2)

    @pl.when(j < split_block)
    def _():
        o_ref[...] = hi_ref[...].astype(BF16)

    @pl.when(j == split_block)
    def _():
        lane = lax.broadcasted_iota(jnp.int32, hi_ref.shape, 1)
        o_ref[...] = jnp.where(lane < split_lane, hi_ref[...], 0.0).astype(BF16)

    @pl.when(j > split_block)
    def _():
        o_ref[...] = jnp.concatenate([lo_ref[:, split_lane:], hi_ref[:, :split_lane]], axis=1).astype(BF16)


def w_in_prep(w_in, tr, tn):
    depth, d, raw = w_in.shape
    pad = Z_GATES - Z_RAW_SPLIT
    assert Z_COLS == raw + pad and pad < tn
    split_block, split_lane = divmod(Z_RAW_SPLIT, tn)
    last = (raw - 1) // tn
    return pl.pallas_call(
        functools.partial(_w_in_prep_kernel, split_block=split_block, split_lane=split_lane),
        grid=(depth, d // tr, Z_COLS // tn),
        in_specs=[
            pl.BlockSpec((None, tr, tn), lambda l, i, j: (l, i, jnp.maximum(j - 1, 0))),
            pl.BlockSpec((None, tr, tn), lambda l, i, j: (l, i, jnp.minimum(j, last))),
        ],
        out_specs=pl.BlockSpec((None, tr, tn), lambda l, i, j: (l, i, j)),
        out_shape=jax.ShapeDtypeStruct((depth, d, Z_COLS), BF16),
        compiler_params=_cparams(("parallel", "parallel", "arbitrary")),
        name="w_in_prep",
    )(w_in, w_in)


def _prep_kernel(bq_ref, bkv_ref, cqa_ref, ckk_ref, cos_a_ref, sin_a_ref, cos_r_ref, sin_r_ref,
                 gbq_ref, gbk_ref, gcq_ref, gckv_ref, wq_ref, wkv_ref,
                 qb_out, kb_out, cq_out, ckv_out, kr_out, *, b_scale, c_scale):
    cos_a, sin_a = cos_a_ref[...], sin_a_ref[...]
    cos_r, sin_r = cos_r_ref[...], sin_r_ref[...]

    for h in range(HEADS):
        sl = slice(h * HEAD_DIM, (h + 1) * HEAD_DIM)
        y = _rms(bq_ref[:, sl].astype(F32), gbq_ref[...])
        qb_out[:, sl] = (_rope(y, cos_a, sin_a) * b_scale).astype(BF16)
    for n in range(B_KV_HEADS):
        sl = slice(n * HEAD_DIM, (n + 1) * HEAD_DIM)
        y = _rms(bkv_ref[:, sl].astype(F32), gbk_ref[...])
        kb_out[:, sl] = _rope(y, cos_a, sin_a).astype(BF16)

    xq = _rms(cqa_ref[...].astype(F32), gcq_ref[...]).astype(BF16)
    cq = jnp.dot(xq, wq_ref[...], preferred_element_type=F32)
    nope_cols = HEADS * C_NOPE
    cq_out[:, :nope_cols] = (cq[:, :nope_cols] * c_scale).astype(BF16)
    for h in range(HEADS):
        sl = slice(nope_cols + h * HEAD_DIM, nope_cols + (h + 1) * HEAD_DIM)
        cq_out[:, sl] = (_rope(cq[:, sl], cos_r, sin_r) * c_scale).astype(BF16)

    xkv = _rms(ckk_ref[:, :C_KV_RANK].astype(F32), gckv_ref[...]).astype(BF16)
    ckv_out[...] = jnp.dot(xkv, wkv_ref[...], preferred_element_type=F32).astype(BF16)
    kr = ckk_ref[:, C_KV_RANK:C_KV_RANK + HEAD_DIM].astype(F32)
    kr_out[...] = _rope(kr, cos_r, sin_r).astype(BF16)


def mixer_prep(z, tables, pos_block, gbq, gbk, gcq, gckv, wq, wkv, tm):
    t = z.shape[0]
    cos_a, sin_a, cos_r, sin_r = tables
    row = lambda w: pl.BlockSpec((tm, w), lambda i: (i, 0))
    tab = pl.BlockSpec((tm, HEAD_DIM), lambda i: (pos_block(i), 0))
    full = lambda a: pl.BlockSpec(a.shape, lambda i: (0,) * a.ndim)
    kern = functools.partial(_prep_kernel, b_scale=HEAD_DIM ** -0.5 * LOG2E,
                             c_scale=(C_NOPE + C_ROPE) ** -0.5 * LOG2E)
    return pl.pallas_call(
        kern,
        grid=(t // tm,),
        in_specs=[
            pl.BlockSpec((tm, 1024), lambda i: (i, Z_BQ // 1024)),
            pl.BlockSpec((tm, 512), lambda i: (i, Z_BK // 512)),
            pl.BlockSpec((tm, 512), lambda i: (i, Z_CQA // 512)),
            pl.BlockSpec((tm, 512), lambda i: (i, Z_CKVA // 512)),
            tab, tab, tab, tab,
            full(gbq), full(gbk), full(gcq), full(gckv), full(wq), full(wkv),
        ],
        out_specs=[row(1024), row(256), row(2048), row(2048), row(HEAD_DIM)],
        out_shape=[
            jax.ShapeDtypeStruct((t, 1024), BF16),
            jax.ShapeDtypeStruct((t, 256), BF16),
            jax.ShapeDtypeStruct((t, 2048), BF16),
            jax.ShapeDtypeStruct((t, 2048), BF16),
            jax.ShapeDtypeStruct((t, HEAD_DIM), BF16),
        ],
        compiler_params=_cparams(("parallel",)),
        name="mixer_prep",
    )(z, z, z, z, cos_a, sin_a, cos_r, sin_r, gbq, gbk, gcq, gckv, wq, wkv)


def _flash(q, k_refs, v_ref, s_refs, nk, tk, chunk_fn=None, bias_fn=None):
    m_rows = q.shape[0]
    dv = v_ref.shape[1]
    ones = jnp.ones((tk, dv), BF16)

    def rows_of(ref, chunk):
        if isinstance(chunk, int):
            return ref[chunk * tk:(chunk + 1) * tk, :]
        return ref[pl.ds(pl.multiple_of(chunk * tk, tk), tk), :]

    def scores(chunk):
        ks = [rows_of(r, chunk) for r in k_refs]
        k = ks[0] if len(ks) == 1 else jnp.concatenate(ks, axis=1)
        return lax.dot_general(q, k, (((1,), (1,)), ((), ())), preferred_element_type=F32)

    def update(pos, chunk, s, carry):
        m, acc = carry
        c = None
        if bias_fn is not None:
            s, c = bias_fn(pos, chunk, s)
        row_max = jnp.max(s, axis=-1, keepdims=True)
        if c is None:
            m_new = jnp.maximum(m, row_max)
            shift = m_new
        else:
            m_new = jnp.maximum(m, row_max + c)
            shift = m_new - c
        p = jnp.exp2((s - shift).astype(BF16))
        alpha = jnp.exp2(m - m_new)
        v_aug = jnp.concatenate([rows_of(v_ref, chunk), ones], axis=1)
        pv = jnp.dot(p, v_aug, preferred_element_type=F32)
        return m_new, alpha * acc + pv

    chunks = [pos if chunk_fn is None else chunk_fn(pos) for pos in range(nk)]
    carry = (jnp.full((m_rows, 1), -jnp.inf, F32), jnp.zeros((m_rows, 2 * dv), F32))
    s_refs[0][...] = scores(chunks[0])
    for pos in range(nk):
        if pos + 1 < nk:
            s_refs[(pos + 1) % 2][...] = scores(chunks[pos + 1])
        carry = update(pos, chunks[pos], s_refs[pos % 2][...], carry)
    m, acc = carry
    return m, acc[:, dv:], acc[:, :dv]


def _attn_kernel(*refs, n_q, n_k, nk, tk, streams):
    n_s = 2 * streams
    q_refs, k_refs = refs[:n_q], refs[n_q:n_q + n_k]
    v_ref, o_ref, s_refs = refs[n_q + n_k], refs[-n_s - 1], refs[-n_s:]
    rows = o_ref.shape[0] // streams
    for t in range(streams):
        sl = slice(t * rows, (t + 1) * rows)
        qs = [r[sl, :] for r in q_refs]
        q = qs[0] if n_q == 1 else jnp.concatenate(qs, axis=1)
        _, l, acc = _flash(q, k_refs, v_ref, s_refs[2 * t:2 * t + 2], nk, tk)
        o_ref[sl, :] = (acc / l).astype(o_ref.dtype)


def _attn_a_kernel(lam_ref, far_ref, q_ref, k_ref, v_ref, bias_ref, g_ref, *rest, tq, tk, nk, r, post_scale,
                   streams):
    n_s = 2 * streams
    o_ref, s_refs = rest[-n_s - 1], rest[-n_s:]
    h = pl.program_id(1)

    for t in range(streams):
        qi = pl.program_id(2) * streams + t
        diag = qi // r
        q = q_ref[t * tq:(t + 1) * tq, :]
        lane = lax.broadcasted_iota(jnp.int32, q.shape, 1)
        zero = jnp.zeros_like(q)
        q2 = jnp.concatenate([jnp.where(lane < A_QK_DIM, q, zero), jnp.where(lane >= A_QK_DIM, q, zero)], axis=0)

        def chunk_fn(pos, diag=diag):
            return (diag + pos) % nk

        def bias_fn(pos, chunk, s, qi=qi, diag=diag):
            tile = jnp.clip(chunk * r - qi, -r - 1, 2) + r + 1
            if pos == 0:
                b = bias_ref[0, tile]
                return jnp.concatenate([s[:tq] + b, s[tq:] + b], axis=0), None
            c = jnp.where(chunk > diag, far_ref[h, 1], far_ref[h, 0])
            edges = ([slice(0, 128)] if pos == 1 else []) + ([slice(tk - 128, tk)] if pos == nk - 1 else [])
            for cols in edges:
                d = bias_ref[0, tile, :, cols] - c
                parts = [s[:, :cols.start], s[:, cols] + jnp.concatenate([d, d], axis=0), s[:, cols.stop:]]
                s = jnp.concatenate([x for x in parts if x.shape[1]], axis=1)
            return s, c

        _, l, acc = _flash(q2, [k_ref], v_ref, s_refs[2 * t:2 * t + 2], nk, tk, chunk_fn, bias_fn)
        o = acc / l
        d = o[:tq] - lam_ref[0] * o[tq:]
        o_ref[t * tq:(t + 1) * tq, :] = (_rms(d, g_ref[...]) * post_scale).astype(o_ref.dtype)


def attention(q_parts, k_parts, v_part, nseq, seq, tq, tk, streams):
    rows = tq * streams
    nq = seq // rows
    in_specs, args = [], []
    for a, cf in q_parts:
        in_specs.append(pl.BlockSpec((rows, HEAD_DIM), lambda s, h, i, cf=cf: (s * nq + i, cf(h))))
        args.append(a)
    for a, cf in list(k_parts) + [v_part]:
        in_specs.append(pl.BlockSpec((seq, HEAD_DIM), lambda s, h, i, cf=cf: (s, cf(h))))
        args.append(a)
    kern = functools.partial(_attn_kernel, n_q=len(q_parts), n_k=len(k_parts), nk=seq // tk, tk=tk,
                             streams=streams)
    return pl.pallas_call(
        kern,
        grid=(nseq, HEADS, nq),
        in_specs=in_specs,
        out_specs=pl.BlockSpec((rows, HEAD_DIM), lambda s, h, i: (s * nq + i, h)),
        out_shape=jax.ShapeDtypeStruct((nseq * seq, HEADS * HEAD_DIM), BF16),
        scratch_shapes=[pltpu.VMEM((tq, tk), F32)] * (2 * streams),
        compiler_params=_cparams(("parallel", "parallel", "arbitrary")),
        name="attention",
    )(*args)


def attention_a(lam, z, bias, g, nseq, seq, tq, tk, post_scale, streams):
    rows = tq * streams
    nq = seq // rows
    r = tk // tq
    far = bias[:, :: r + 3, 0, 0]
    in_specs = [
        pl.BlockSpec(memory_space=pltpu.SMEM),
        pl.BlockSpec(memory_space=pltpu.SMEM),
        pl.BlockSpec((rows, HEAD_DIM), lambda s, h, i: (s * nq + i, Z_AQ // HEAD_DIM + h)),
        pl.BlockSpec((seq, HEAD_DIM), lambda s, h, i: (s, Z_AK // HEAD_DIM + h)),
        pl.BlockSpec((seq, HEAD_DIM), lambda s, h, i: (s, Z_AV // HEAD_DIM + h)),
        pl.BlockSpec((1, r + 4, tq, tk), lambda s, h, i: (h, 0, 0, 0)),
        pl.BlockSpec((1, HEAD_DIM), lambda s, h, i: (0, 0)),
    ]
    kern = functools.partial(_attn_a_kernel, tq=tq, tk=tk, nk=seq // tk, r=r, post_scale=post_scale,
                             streams=streams)
    return pl.pallas_call(
        kern,
        grid=(nseq, HEADS, nq),
        in_specs=in_specs,
        out_specs=pl.BlockSpec((rows, HEAD_DIM), lambda s, h, i: (s * nq + i, h)),
        out_shape=jax.ShapeDtypeStruct((nseq * seq, HEADS * HEAD_DIM), BF16),
        scratch_shapes=[pltpu.VMEM((2 * tq, tk), F32)] * (2 * streams),
        compiler_params=_cparams(("parallel", "parallel", "arbitrary")),
        name="attention_a",
    )(lam, far, z, z, z, bias, g)


def _merge_kernel(oa_ref, ob_ref, oc_ref, wa_ref, wb_ref, wc_ref, ga_ref, gb_ref, gc_ref, o_ref):
    def branch(o_r, w_r, g_r):
        y = jnp.dot(o_r[...], w_r[...], preferred_element_type=F32)
        return jax.nn.sigmoid(g_r[...].astype(F32)) * y

    merged = branch(oa_ref, wa_ref, ga_ref) + branch(ob_ref, wb_ref, gb_ref) + branch(oc_ref, wc_ref, gc_ref)
    o_ref[...] = merged.astype(o_ref.dtype)


def gated_merge(oa, ob, oc, wa, wb, wc, z, tm, tn):
    t, k = oa.shape
    n = wa.shape[1]
    o_spec = pl.BlockSpec((tm, k), lambda i, j: (i, 0))
    w_spec = pl.BlockSpec((k, tn), lambda i, j: (0, j))
    gate = lambda b: pl.BlockSpec((tm, tn), lambda i, j: (i, (Z_GATES + b * n) // tn + j))
    return pl.pallas_call(
        _merge_kernel,
        grid=(t // tm, n // tn),
        in_specs=[o_spec, o_spec, o_spec, w_spec, w_spec, w_spec, gate(0), gate(1), gate(2)],
        out_specs=pl.BlockSpec((tm, tn), lambda i, j: (i, j)),
        out_shape=jax.ShapeDtypeStruct((t, n), BF16),
        compiler_params=_cparams(("parallel", "arbitrary")),
        name="gated_merge",
    )(oa, ob, oc, wa, wb, wc, z, z, z)


def _mm_postnorm_kernel(a_ref, w_ref, g_ref, x_ref, o_ref):
    y = jnp.dot(a_ref[...], w_ref[...], preferred_element_type=F32)
    o_ref[...] = x_ref[...] + _rms(y, g_ref[...])


def mm_postnorm_residual(a, w, g, x, tm):
    t, k = a.shape
    n = w.shape[1]
    return pl.pallas_call(
        _mm_postnorm_kernel,
        grid=(t // tm,),
        in_specs=[
            pl.BlockSpec((tm, k), lambda i: (i, 0)),
            pl.BlockSpec((k, n), lambda i: (0, 0)),
            pl.BlockSpec((1, n), lambda i: (0, 0)),
            pl.BlockSpec((tm, n), lambda i: (i, 0)),
        ],
        out_specs=pl.BlockSpec((tm, n), lambda i: (i, 0)),
        out_shape=jax.ShapeDtypeStruct((t, n), F32),
        compiler_params=_cparams(("parallel",)),
        name="mm_postnorm_residual",
    )(a, w, g, x)


def _xattn_kernel(x_ref, gpre_ref, wq_ref, kv_ref, wo_ref, gpost_ref, o_ref, *, q_scale):
    x = x_ref[...]
    h = _rms(x, gpre_ref[...]).astype(BF16)
    q = (jnp.dot(h, wq_ref[...], preferred_element_type=F32) * q_scale).astype(BF16)
    kv_cols = X_HEADS * HEAD_DIM
    outs = []
    for hd in range(X_HEADS):
        sl = slice(hd * HEAD_DIM, (hd + 1) * HEAD_DIM)
        kh = kv_ref[0, :, sl]
        vh = kv_ref[0, :, kv_cols + hd * HEAD_DIM:kv_cols + (hd + 1) * HEAD_DIM]
        s = lax.dot_general(q[:, sl], kh, (((1,), (1,)), ((), ())), preferred_element_type=F32)
        p = jnp.exp2(s - jnp.max(s, axis=-1, keepdims=True))
        l = jnp.sum(p, axis=-1, keepdims=True)
        outs.append((jnp.dot(p.astype(BF16), vh, preferred_element_type=F32) / l).astype(BF16))
    o = jnp.concatenate(outs, axis=1)
    y = jnp.dot(o, wo_ref[...], preferred_element_type=F32)
    o_ref[...] = x + _rms(y, gpost_ref[...])


def memory_xattn(x, gpre, wq, memkv, mem_block, wo, gpost, tm):
    t, d = x.shape
    full = lambda a: pl.BlockSpec(a.shape, lambda i: (0,) * a.ndim)
    kern = functools.partial(_xattn_kernel, q_scale=HEAD_DIM ** -0.5 * LOG2E)
    return pl.pallas_call(
        kern,
        grid=(t // tm,),
        in_specs=[
            pl.BlockSpec((tm, d), lambda i: (i, 0)),
            full(gpre), full(wq),
            pl.BlockSpec((1,) + memkv.shape[1:], lambda i: (mem_block(i), 0, 0)),
            full(wo), full(gpost),
        ],
        out_specs=pl.BlockSpec((tm, d), lambda i: (i, 0)),
        out_shape=jax.ShapeDtypeStruct((t, d), F32),
        compiler_params=_cparams(("parallel",)),
        name="memory_xattn",
    )(x, gpre, wq, memkv, wo, gpost)


def _ffn_kernel(x_ref, gpre_ref, wg_ref, wu_ref, wd_ref, gpost_ref, o_ref, h_ref, acc_ref):
    f = pl.program_id(1)

    @pl.when(f == 0)
    def _():
        h_ref[...] = _rms(x_ref[...], gpre_ref[...]).astype(BF16)
        acc_ref[...] = jnp.zeros_like(acc_ref)

    h = h_ref[...]
    a = jnp.dot(h, wg_ref[...], preferred_element_type=F32)
    b = jnp.dot(h, wu_ref[...], preferred_element_type=F32)
    t = (a * jax.nn.sigmoid(a) * b).astype(BF16)
    acc_ref[...] += jnp.dot(t, wd_ref[...], preferred_element_type=F32)

    @pl.when(f == pl.num_programs(1) - 1)
    def _():
        o_ref[...] = x_ref[...] + _rms(acc_ref[...], gpost_ref[...])


def swiglu_ffn(x, gpre, wg, wu, wd, layer, gpost, tm, tf):
    t, d = x.shape
    ff = wg.shape[2]
    return pl.pallas_call(
        _ffn_kernel,
        grid=(t // tm, ff // tf),
        in_specs=[
            pl.BlockSpec((tm, d), lambda i, f: (i, 0)),
            pl.BlockSpec((1, d), lambda i, f: (0, 0)),
            pl.BlockSpec((None, d, tf), lambda i, f: (layer, 0, f)),
            pl.BlockSpec((None, d, tf), lambda i, f: (layer, 0, f)),
            pl.BlockSpec((None, tf, d), lambda i, f: (layer, f, 0)),
            pl.BlockSpec((1, d), lambda i, f: (0, 0)),
        ],
        out_specs=pl.BlockSpec((tm, d), lambda i, f: (i, 0)),
        out_shape=jax.ShapeDtypeStruct((t, d), F32),
        scratch_shapes=[pltpu.VMEM((tm, d), BF16), pltpu.VMEM((tm, d), F32)],
        compiler_params=_cparams(("parallel", "arbitrary")),
        name="swiglu_ffn",
    )(x, gpre, wg, wu, wd, gpost)


def _rotary_tables(seq):
    pos = jnp.arange(seq)

    def cs(p):
        inv = ROPE_THETA ** (-jnp.arange(0, 64, 2, dtype=F32) / 64)
        ang = p.astype(F32)[:, None] * inv[None, :]
        c, s = jnp.cos(ang), jnp.sin(ang)
        return jnp.concatenate([c, c], axis=-1), jnp.concatenate([-s, s], axis=-1)

    c_row, s_row = cs(pos // GRID_W)
    c_col, s_col = cs(pos % GRID_W)
    c_tok, s_tok = cs(pos)
    return (jnp.concatenate([c_row, c_col], axis=-1), jnp.concatenate([s_row, s_col], axis=-1),
            jnp.concatenate([c_tok, c_tok], axis=-1), jnp.concatenate([s_tok, s_tok], axis=-1))


def _t5_bucket(rel):
    nb = NUM_BUCKETS // 2
    max_exact = nb // 2
    ret = (rel > 0).astype(jnp.int32) * nb
    n = jnp.abs(rel)
    nf = jnp.maximum(n, 1).astype(F32)
    large = max_exact + (jnp.log(nf / max_exact) / math.log(MAX_DISTANCE / max_exact)
                         * (nb - max_exact)).astype(jnp.int32)
    large = jnp.minimum(large, nb - 1)
    return ret + jnp.where(n < max_exact, n, large)


T5_FAR = 91


def _bias_kernel(tab_ref, o_ref, *, tq, tk, r):
    h = pl.program_id(0)
    c_left = tab_ref[h, NUM_BUCKETS // 2 - 1] * LOG2E
    c_right = tab_ref[h, NUM_BUCKETS - 1] * LOG2E
    for t in range(r + 4):
        off = (t - (r + 1)) * tq
        lo = min(max((-off - T5_FAR + 1) // 128 * 128, 0), tk)
        hi = min(max(-((off - tq - T5_FAR + 1) // 128) * 128, 0), tk)
        if lo > 0:
            o_ref[0, t, :, :lo] = jnp.full((tq, lo), c_left, F32)
        if hi < tk:
            o_ref[0, t, :, hi:] = jnp.full((tq, tk - hi), c_right, F32)
        if hi > lo:
            rel = (off + lo + lax.broadcasted_iota(jnp.int32, (tq, hi - lo), 1)
                   - lax.broadcasted_iota(jnp.int32, (tq, hi - lo), 0))
            bucket = _t5_bucket(rel)
            v = jnp.full((tq, hi - lo), tab_ref[h, 0], F32)
            for b in range(1, NUM_BUCKETS):
                v = jnp.where(bucket == b, tab_ref[h, b], v)
            o_ref[0, t, :, lo:hi] = v * LOG2E


def _bias_tiles(rel_bias, tq, tk):
    assert tq >= 128 and tk % tq == 0
    n = np.arange(T5_FAR, 1 << 16, dtype=np.float32)
    far_bucket = np.minimum(8 + (np.log(n / 8) / math.log(MAX_DISTANCE / 8) * 8).astype(np.int32), 15)
    assert NUM_BUCKETS == 32 and far_bucket.min() == 15
    r = tk // tq
    return pl.pallas_call(
        functools.partial(_bias_kernel, tq=tq, tk=tk, r=r),
        grid=(HEADS,),
        in_specs=[pl.BlockSpec(memory_space=pltpu.SMEM)],
        out_specs=pl.BlockSpec((1, r + 4, tq, tk), lambda h: (h, 0, 0, 0)),
        out_shape=jax.ShapeDtypeStruct((HEADS, r + 4, tq, tk), F32),
        compiler_params=_cparams(("parallel",)),
        name="bias_tiles",
    )(rel_bias.T.astype(F32))


def kernel(x_prompt, x_sample, mem_prompt, mem_sample, rel_bias, g_mix_pre, g_mix_post, w_in, lam_q1, lam_k1, lam_q2, lam_k2, g_a_out, g_b_q, g_b_k, g_c_q, g_c_kv, w_c_q_up, w_c_kv_up, w_br_a, w_br_b, w_br_c, w_mix_out, g_x_pre, g_x_post, g_mem, w_x_q, w_x_kv, w_x_out, g_ffn_pre, g_ffn_post, w_ffn_gate, w_ffn_up, w_ffn_down):
    depth = w_in.shape[0]
    d = x_prompt.shape[-1]
    tm_row = 512
    tq_a, tk_a = 256, 1024
    tq, tk = 512, 1024

    groups = [(x.reshape(-1, d), m, x.shape[0], x.shape[1])
              for x, m in ((x_prompt, mem_prompt), (x_sample, mem_sample))]
    tables = _rotary_tables(max(g[3] for g in groups))
    bias = _bias_tiles(rel_bias, tq_a, tk_a)
    row = lambda g: g.reshape(1, -1).astype(F32)
    in_scale = jnp.concatenate([jnp.full((1, Z_AK), A_QK_DIM ** -0.5 * LOG2E, F32),
                                jnp.ones((1, Z_COLS - Z_AK), F32)], axis=-1)
    ones_kv = jnp.ones((1, w_x_kv.shape[-1]), F32)
    kv_group = HEADS // B_KV_HEADS

    w_in_p = w_in_prep(w_in, tr=1024, tn=512)
    w_xkv_all = w_x_kv.astype(BF16)
    w_fg, w_fu, w_fd = w_ffn_gate.astype(BF16), w_ffn_up.astype(BF16), w_ffn_down.astype(BF16)

    xs = [g[0] for g in groups]
    for l in range(depth):
        lam_init = 0.8 - 0.6 * math.exp(-0.3 * l)
        lam = (jnp.exp(jnp.sum(lam_q1[l].astype(F32) * lam_k1[l].astype(F32)))
               - jnp.exp(jnp.sum(lam_q2[l].astype(F32) * lam_k2[l].astype(F32))) + lam_init).reshape(1)

        wq3 = w_c_q_up[l].reshape(C_Q_RANK, HEADS, C_NOPE + C_ROPE)
        w_cq = jnp.concatenate(
            [wq3[..., :C_NOPE].reshape(C_Q_RANK, HEADS * C_NOPE),
             jnp.pad(wq3[..., C_NOPE:], ((0, 0), (0, 0), (0, HEAD_DIM - C_ROPE))).reshape(
                 C_Q_RANK, HEADS * HEAD_DIM)], axis=-1).astype(BF16)
        w_ckv = w_c_kv_up[l].astype(BF16)
        w_a, w_b, w_c = w_br_a[l].astype(BF16), w_br_b[l].astype(BF16), w_br_c[l].astype(BF16)
        w_mix = w_mix_out[l].astype(BF16)
        w_xq, w_xo = w_x_q[l].astype(BF16), w_x_out[l].astype(BF16)

        for gi, (_, mem, nseq, seq) in enumerate(groups):
            x = xs[gi]
            blocks_per_seq = seq // tm_row
            streams = 2 if seq // tk <= 4 else 1
            z = rms_matmul(x, row(g_mix_pre[l]), w_in_p, l, in_scale, tm=1024, tn=512)
            qb, kb, cq, ckv, kr = mixer_prep(z, tables, lambda i: i % blocks_per_seq, row(g_b_q[l]), row(g_b_k[l]),
                                             row(g_c_q[l]), row(g_c_kv[l]), w_cq, w_ckv, tm_row)
            oa = attention_a(lam, z, bias, row(g_a_out[l]), nseq, seq, tq_a, tk_a, 1.0 - lam_init, streams)
            ob = attention([(qb, lambda h: h)],
                           [(kb, lambda h: h // kv_group)],
                           (z, lambda h: Z_BV // HEAD_DIM + h // kv_group),
                           nseq, seq, tq, tk, streams)
            oc = attention([(cq, lambda h: h), (cq, lambda h: HEADS + h)],
                           [(ckv, lambda h: 2 * h), (kr, lambda h: 0)],
                           (ckv, lambda h: 2 * h + 1),
                           nseq, seq, tq, tk, streams)
            merged = gated_merge(oa, ob, oc, w_a, w_b, w_c, z, tm=1024, tn=512)
            x = mm_postnorm_residual(merged, w_mix, row(g_mix_post[l]), x, tm=tm_row)

            mem2 = mem.reshape(nseq * MEM_TOKENS, d)
            memkv = rms_matmul(mem2, row(g_mem[l]), w_xkv_all, l, ones_kv, tm=mem2.shape[0], tn=512)
            memkv = memkv.reshape(nseq, MEM_TOKENS, -1)
            x = memory_xattn(x, row(g_x_pre[l]), w_xq, memkv, lambda i: i // blocks_per_seq, w_xo,
                             row(g_x_post[l]), tm_row)
            xs[gi] = swiglu_ffn(x, row(g_ffn_pre[l]), w_fg, w_fu, w_fd, l, row(g_ffn_post[l]), tm=tm_row, tf=512)

    return tuple(x.reshape(g[2], g[3], d) for x, g in zip(xs, groups))
```

```python
import functools
import math

import jax
import jax.numpy as jnp
import numpy as np
from jax import lax
from jax.experimental import pallas as pl
from jax.experimental.pallas import tpu as pltpu

F32 = jnp.float32
BF16 = jnp.bfloat16

EPS = 1e-6
ROPE_THETA = 10000.0
LOG2E = 1.4426950408889634
GRID_W = 64
NUM_BUCKETS = 32
MAX_DISTANCE = 128

D_MODEL = 2048
HEADS = 8
HEAD_DIM = 128
A_QK_DIM = 64
B_KV_HEADS = 2
C_Q_RANK = 512
C_KV_RANK = 256
C_NOPE = 128
C_ROPE = 64
X_HEADS = 4
MEM_TOKENS = 256

Z_AQ, Z_AK, Z_AV, Z_BQ, Z_BK, Z_BV, Z_CQA, Z_CKVA, Z_CKR = 0, 1024, 2048, 3072, 4096, 4352, 4608, 5120, 5376
Z_RAW_SPLIT = 5440
Z_GATES = 5632
Z_COLS = Z_GATES + 3 * D_MODEL

VMEM_LIMIT = 56 * 1024 * 1024


def _cparams(sem):
    return pltpu.CompilerParams(dimension_semantics=sem, vmem_limit_bytes=VMEM_LIMIT)


def _rms(x, g):
    return x * lax.rsqrt(jnp.mean(x * x, axis=-1, keepdims=True) + EPS) * g


def _swap32(x):
    lane = lax.broadcasted_iota(jnp.int32, x.shape, 1)
    return jnp.where((lane % 64) < 32, pltpu.roll(x, 96, 1), pltpu.roll(x, 32, 1))


def _rope(x, cos, sin_signed):
    return x * cos + _swap32(x) * sin_signed


def _rms_matmul_kernel(x_ref, g_ref, w_ref, cs_ref, o_ref, xn_ref):
    @pl.when(pl.program_id(1) == 0)
    def _():
        xn_ref[...] = _rms(x_ref[...].astype(F32), g_ref[...]).astype(BF16)

    acc = jnp.dot(xn_ref[...], w_ref[...], preferred_element_type=F32)
    o_ref[...] = (acc * cs_ref[...]).astype(o_ref.dtype)


def rms_matmul(x, g, w, layer, colscale, tm, tn, out_dtype=BF16):
    m, k = x.shape
    n = w.shape[2]
    return pl.pallas_call(
        _rms_matmul_kernel,
        grid=(m // tm, n // tn),
        in_specs=[
            pl.BlockSpec((tm, k), lambda i, j: (i, 0)),
            pl.BlockSpec((1, k), lambda i, j: (0, 0)),
            pl.BlockSpec((None, k, tn), lambda i, j: (layer, 0, j)),
            pl.BlockSpec((1, tn), lambda i, j: (0, j)),
        ],
        out_specs=pl.BlockSpec((tm, tn), lambda i, j: (i, j)),
        out_shape=jax.ShapeDtypeStruct((m, n), out_dtype),
        scratch_shapes=[pltpu.VMEM((tm, k), BF16)],
        compiler_params=_cparams(("parallel", "arbitrary")),
        name="rms_matmul",
    )(x, g, w, colscale)


def _w_in_prep_kernel(lo_ref, hi_ref, o_ref, *, split_block, split_row):
    j = pl.program_id(2)

    def emit(rows):
        o_ref[...] = rows.T.astype(BF16)

    @pl.when(j < split_block)
    def _():
        emit(hi_ref[...])

    @pl.when(j == split_block)
    def _():
        row = lax.broadcasted_iota(jnp.int32, hi_ref.shape, 0)
        emit(jnp.where(row < split_row, hi_ref[...], 0.0))

    @pl.when(j > split_block)
    def _():
        emit(jnp.concatenate([lo_ref[split_row:, :], hi_ref[:split_row, :]], axis=0))


def w_in_prep(w_in_t, tr, tn):
    depth, raw, d = w_in_t.shape
    pad = Z_GATES - Z_RAW_SPLIT
    assert Z_COLS == raw + pad and pad < tn
    split_block, split_row = divmod(Z_RAW_SPLIT, tn)
    assert split_row % 8 == 0
    last = (raw - 1) // tn
    return pl.pallas_call(
        functools.partial(_w_in_prep_kernel, split_block=split_block, split_row=split_row),
        grid=(depth, d // tr, Z_COLS // tn),
        in_specs=[
            pl.BlockSpec((None, tn, tr), lambda l, i, j: (l, jnp.maximum(j - 1, 0), i)),
            pl.BlockSpec((None, tn, tr), lambda l, i, j: (l, jnp.minimum(j, last), i)),
        ],
        out_specs=pl.BlockSpec((None, tr, tn), lambda l, i, j: (l, i, j)),
        out_shape=jax.ShapeDtypeStruct((depth, d, Z_COLS), BF16),
        compiler_params=_cparams(("parallel", "parallel", "arbitrary")),
        name="w_in_prep",
    )(w_in_t, w_in_t)


def _prep_kernel(bq_ref, bkv_ref, cqa_ref, ckk_ref, cos_a_ref, sin_a_ref, cos_r_ref, sin_r_ref,
                 gbq_ref, gbk_ref, gcq_ref, gckv_ref, wq_ref, wkv_ref,
                 qb_out, kb_out, cq_out, ckv_out, kr_out, *, b_scale, c_scale):
    cos_a, sin_a = cos_a_ref[...], sin_a_ref[...]
    cos_r, sin_r = cos_r_ref[...], sin_r_ref[...]

    for h in range(HEADS):
        sl = slice(h * HEAD_DIM, (h + 1) * HEAD_DIM)
        y = _rms(bq_ref[:, sl].astype(F32), gbq_ref[...])
        qb_out[:, sl] = (_rope(y, cos_a, sin_a) * b_scale).astype(BF16)
    for n in range(B_KV_HEADS):
        sl = slice(n * HEAD_DIM, (n + 1) * HEAD_DIM)
        y = _rms(bkv_ref[:, sl].astype(F32), gbk_ref[...])
        kb_out[:, sl] = _rope(y, cos_a, sin_a).astype(BF16)

    xq = _rms(cqa_ref[...].astype(F32), gcq_ref[...]).astype(BF16)
    cq = jnp.dot(xq, wq_ref[...], preferred_element_type=F32)
    nope_cols = HEADS * C_NOPE
    cq_out[:, :nope_cols] = (cq[:, :nope_cols] * c_scale).astype(BF16)
    for h in range(HEADS):
        sl = slice(nope_cols + h * HEAD_DIM, nope_cols + (h + 1) * HEAD_DIM)
        cq_out[:, sl] = (_rope(cq[:, sl], cos_r, sin_r) * c_scale).astype(BF16)

    xkv = _rms(ckk_ref[:, :C_KV_RANK].astype(F32), gckv_ref[...]).astype(BF16)
    ckv_out[...] = jnp.dot(xkv, wkv_ref[...], preferred_element_type=F32).astype(BF16)
    kr = ckk_ref[:, C_KV_RANK:C_KV_RANK + HEAD_DIM].astype(F32)
    kr_out[...] = _rope(kr, cos_r, sin_r).astype(BF16)


def mixer_prep(z, tables, pos_block, gbq, gbk, gcq, gckv, wq, wkv, tm):
    t = z.shape[0]
    cos_a, sin_a, cos_r, sin_r = tables
    row = lambda w: pl.BlockSpec((tm, w), lambda i: (i, 0))
    tab = pl.BlockSpec((tm, HEAD_DIM), lambda i: (pos_block(i), 0))
    full = lambda a: pl.BlockSpec(a.shape, lambda i: (0,) * a.ndim)
    kern = functools.partial(_prep_kernel, b_scale=HEAD_DIM ** -0.5 * LOG2E,
                             c_scale=(C_NOPE + C_ROPE) ** -0.5 * LOG2E)
    return pl.pallas_call(
        kern,
        grid=(t // tm,),
        in_specs=[
            pl.BlockSpec((tm, 1024), lambda i: (i, Z_BQ // 1024)),
            pl.BlockSpec((tm, 512), lambda i: (i, Z_BK // 512)),
            pl.BlockSpec((tm, 512), lambda i: (i, Z_CQA // 512)),
            pl.BlockSpec((tm, 512), lambda i: (i, Z_CKVA // 512)),
            tab, tab, tab, tab,
            full(gbq), full(gbk), full(gcq), full(gckv), full(wq), full(wkv),
        ],
        out_specs=[row(1024), row(256), row(2048), row(2048), row(HEAD_DIM)],
        out_shape=[
            jax.ShapeDtypeStruct((t, 1024), BF16),
            jax.ShapeDtypeStruct((t, 256), BF16),
            jax.ShapeDtypeStruct((t, 2048), BF16),
            jax.ShapeDtypeStruct((t, 2048), BF16),
            jax.ShapeDtypeStruct((t, HEAD_DIM), BF16),
        ],
        compiler_params=_cparams(("parallel",)),
        name="mixer_prep",
    )(z, z, z, z, cos_a, sin_a, cos_r, sin_r, gbq, gbk, gcq, gckv, wq, wkv)


def _flash(q, k_refs, v_ref, s_refs, nk, tk, chunk_fn=None, bias_fn=None):
    m_rows = q.shape[0]
    dv = v_ref.shape[1]
    ones = jnp.ones((tk, dv), BF16)

    def rows_of(ref, chunk):
        if isinstance(chunk, int):
            return ref[chunk * tk:(chunk + 1) * tk, :]
        return ref[pl.ds(pl.multiple_of(chunk * tk, tk), tk), :]

    def scores(chunk):
        ks = [rows_of(r, chunk) for r in k_refs]
        k = ks[0] if len(ks) == 1 else jnp.concatenate(ks, axis=1)
        return lax.dot_general(q, k, (((1,), (1,)), ((), ())), preferred_element_type=F32)

    def update(pos, chunk, s, carry):
        m, acc = carry
        c = None
        if bias_fn is not None:
            s, c = bias_fn(pos, chunk, s)
        row_max = jnp.max(s, axis=-1, keepdims=True)
        if c is None:
            m_new = jnp.maximum(m, row_max)
            shift = m_new
        else:
            m_new = jnp.maximum(m, row_max + c)
            shift = m_new - c
        p = jnp.exp2((s - shift).astype(BF16))
        alpha = jnp.exp2(m - m_new)
        v_aug = jnp.concatenate([rows_of(v_ref, chunk), ones], axis=1)
        pv = jnp.dot(p, v_aug, preferred_element_type=F32)
        return m_new, alpha * acc + pv

    chunks = [pos if chunk_fn is None else chunk_fn(pos) for pos in range(nk)]
    carry = (jnp.full((m_rows, 1), -jnp.inf, F32), jnp.zeros((m_rows, 2 * dv), F32))
    s_refs[0][...] = scores(chunks[0])
    for pos in range(nk):
        if pos + 1 < nk:
            s_refs[(pos + 1) % 2][...] = scores(chunks[pos + 1])
        carry = update(pos, chunks[pos], s_refs[pos % 2][...], carry)
    m, acc = carry
    return m, acc[:, dv:], acc[:, :dv]


def _attn_kernel(*refs, n_q, n_k, nk, tk, streams):
    n_s = 2 * streams
    q_refs, k_refs = refs[:n_q], refs[n_q:n_q + n_k]
    v_ref, o_ref, s_refs = refs[n_q + n_k], refs[-n_s - 1], refs[-n_s:]
    rows = o_ref.shape[0] // streams
    for t in range(streams):
        sl = slice(t * rows, (t + 1) * rows)
        qs = [r[sl, :] for r in q_refs]
        q = qs[0] if n_q == 1 else jnp.concatenate(qs, axis=1)
        _, l, acc = _flash(q, k_refs, v_ref, s_refs[2 * t:2 * t + 2], nk, tk)
        o_ref[sl, :] = (acc / l).astype(o_ref.dtype)


def _attn_a_kernel(lam_ref, far_ref, q_ref, k_ref, v_ref, bias_ref, g_ref, *rest, tq, tk, nk, r, post_scale,
                   streams):
    n_s = 2 * streams
    o_ref, s_refs = rest[-n_s - 1], rest[-n_s:]
    h = pl.program_id(1)

    for t in range(streams):
        qi = pl.program_id(2) * streams + t
        diag = qi // r
        q = q_ref[t * tq:(t + 1) * tq, :]
        lane = lax.broadcasted_iota(jnp.int32, q.shape, 1)
        zero = jnp.zeros_like(q)
        q2 = jnp.concatenate([jnp.where(lane < A_QK_DIM, q, zero), jnp.where(lane >= A_QK_DIM, q, zero)], axis=0)

        def chunk_fn(pos, diag=diag):
            return (diag + pos) % nk

        def bias_fn(pos, chunk, s, qi=qi, diag=diag):
            tile = jnp.clip(chunk * r - qi, -r - 1, 2) + r + 1
            if pos == 0:
                b = bias_ref[0, tile]
                return jnp.concatenate([s[:tq] + b, s[tq:] + b], axis=0), None
            c = jnp.where(chunk > diag, far_ref[h, 1], far_ref[h, 0])
            edges = ([slice(0, 128)] if pos == 1 else []) + ([slice(tk - 128, tk)] if pos == nk - 1 else [])
            for cols in edges:
                d = bias_ref[0, tile, :, cols] - c
                parts = [s[:, :cols.start], s[:, cols] + jnp.concatenate([d, d], axis=0), s[:, cols.stop:]]
                s = jnp.concatenate([x for x in parts if x.shape[1]], axis=1)
            return s, c

        _, l, acc = _flash(q2, [k_ref], v_ref, s_refs[2 * t:2 * t + 2], nk, tk, chunk_fn, bias_fn)
        o = acc / l
        d = o[:tq] - lam_ref[0] * o[tq:]
        o_ref[t * tq:(t + 1) * tq, :] = (_rms(d, g_ref[...]) * post_scale).astype(o_ref.dtype)


def attention(q_parts, k_parts, v_part, nseq, seq, tq, tk, streams):
    rows = tq * streams
    nq = seq // rows
    in_specs, args = [], []
    for a, cf in q_parts:
        in_specs.append(pl.BlockSpec((rows, HEAD_DIM), lambda s, h, i, cf=cf: (s * nq + i, cf(h))))
        args.append(a)
    for a, cf in list(k_parts) + [v_part]:
        in_specs.append(pl.BlockSpec((seq, HEAD_DIM), lambda s, h, i, cf=cf: (s, cf(h))))
        args.append(a)
    kern = functools.partial(_attn_kernel, n_q=len(q_parts), n_k=len(k_parts), nk=seq // tk, tk=tk,
                             streams=streams)
    return pl.pallas_call(
        kern,
        grid=(nseq, HEADS, nq),
        in_specs=in_specs,
        out_specs=pl.BlockSpec((rows, HEAD_DIM), lambda s, h, i: (s * nq + i, h)),
        out_shape=jax.ShapeDtypeStruct((nseq * seq, HEADS * HEAD_DIM), BF16),
        scratch_shapes=[pltpu.VMEM((tq, tk), F32)] * (2 * streams),
        compiler_params=_cparams(("parallel", "parallel", "arbitrary")),
        name="attention",
    )(*args)


def attention_a(lam, z, bias, g, nseq, seq, tq, tk, post_scale, streams):
    rows = tq * streams
    nq = seq // rows
    r = tk // tq
    far = bias[:, :: r + 3, 0, 0]
    in_specs = [
        pl.BlockSpec(memory_space=pltpu.SMEM),
        pl.BlockSpec(memory_space=pltpu.SMEM),
        pl.BlockSpec((rows, HEAD_DIM), lambda s, h, i: (s * nq + i, Z_AQ // HEAD_DIM + h)),
        pl.BlockSpec((seq, HEAD_DIM), lambda s, h, i: (s, Z_AK // HEAD_DIM + h)),
        pl.BlockSpec((seq, HEAD_DIM), lambda s, h, i: (s, Z_AV // HEAD_DIM + h)),
        pl.BlockSpec((1, r + 4, tq, tk), lambda s, h, i: (h, 0, 0, 0)),
        pl.BlockSpec((1, HEAD_DIM), lambda s, h, i: (0, 0)),
    ]
    kern = functools.partial(_attn_a_kernel, tq=tq, tk=tk, nk=seq // tk, r=r, post_scale=post_scale,
                             streams=streams)
    return pl.pallas_call(
        kern,
        grid=(nseq, HEADS, nq),
        in_specs=in_specs,
        out_specs=pl.BlockSpec((rows, HEAD_DIM), lambda s, h, i: (s * nq + i, h)),
        out_shape=jax.ShapeDtypeStruct((nseq * seq, HEADS * HEAD_DIM), BF16),
        scratch_shapes=[pltpu.VMEM((2 * tq, tk), F32)] * (2 * streams),
        compiler_params=_cparams(("parallel", "parallel", "arbitrary")),
        name="attention_a",
    )(lam, far, z, z, z, bias, g)


def _merge_kernel(oa_ref, ob_ref, oc_ref, wa_ref, wb_ref, wc_ref, ga_ref, gb_ref, gc_ref, o_ref):
    def branch(o_r, w_r, g_r):
        y = jnp.dot(o_r[...], w_r[...], preferred_element_type=F32)
        return jax.nn.sigmoid(g_r[...].astype(F32)) * y

    merged = branch(oa_ref, wa_ref, ga_ref) + branch(ob_ref, wb_ref, gb_ref) + branch(oc_ref, wc_ref, gc_ref)
    o_ref[...] = merged.astype(o_ref.dtype)


def gated_merge(oa, ob, oc, wa, wb, wc, z, tm, tn):
    t, k = oa.shape
    n = wa.shape[1]
    o_spec = pl.BlockSpec((tm, k), lambda i, j: (i, 0))
    w_spec = pl.BlockSpec((k, tn), lambda i, j: (0, j))
    gate = lambda b: pl.BlockSpec((tm, tn), lambda i, j: (i, (Z_GATES + b * n) // tn + j))
    return pl.pallas_call(
        _merge_kernel,
        grid=(t // tm, n // tn),
        in_specs=[o_spec, o_spec, o_spec, w_spec, w_spec, w_spec, gate(0), gate(1), gate(2)],
        out_specs=pl.BlockSpec((tm, tn), lambda i, j: (i, j)),
        out_shape=jax.ShapeDtypeStruct((t, n), BF16),
        compiler_params=_cparams(("parallel", "arbitrary")),
        name="gated_merge",
    )(oa, ob, oc, wa, wb, wc, z, z, z)


def _mm_postnorm_kernel(a_ref, w_ref, g_ref, x_ref, o_ref):
    y = jnp.dot(a_ref[...], w_ref[...], preferred_element_type=F32)
    o_ref[...] = x_ref[...] + _rms(y, g_ref[...])


def mm_postnorm_residual(a, w, g, x, tm):
    t, k = a.shape
    n = w.shape[1]
    return pl.pallas_call(
        _mm_postnorm_kernel,
        grid=(t // tm,),
        in_specs=[
            pl.BlockSpec((tm, k), lambda i: (i, 0)),
            pl.BlockSpec((k, n), lambda i: (0, 0)),
            pl.BlockSpec((1, n), lambda i: (0, 0)),
            pl.BlockSpec((tm, n), lambda i: (i, 0)),
        ],
        out_specs=pl.BlockSpec((tm, n), lambda i: (i, 0)),
        out_shape=jax.ShapeDtypeStruct((t, n), F32),
        compiler_params=_cparams(("parallel",)),
        name="mm_postnorm_residual",
    )(a, w, g, x)


def _xattn_kernel(x_ref, gpre_ref, wq_ref, kv_ref, wo_ref, gpost_ref, o_ref, *, q_scale):
    x = x_ref[...]
    h = _rms(x, gpre_ref[...]).astype(BF16)
    q = (jnp.dot(h, wq_ref[...], preferred_element_type=F32) * q_scale).astype(BF16)
    kv_cols = X_HEADS * HEAD_DIM
    outs = []
    for hd in range(X_HEADS):
        sl = slice(hd * HEAD_DIM, (hd + 1) * HEAD_DIM)
        kh = kv_ref[0, :, sl]
        vh = kv_ref[0, :, kv_cols + hd * HEAD_DIM:kv_cols + (hd + 1) * HEAD_DIM]
        s = lax.dot_general(q[:, sl], kh, (((1,), (1,)), ((), ())), preferred_element_type=F32)
        p = jnp.exp2(s - jnp.max(s, axis=-1, keepdims=True))
        l = jnp.sum(p, axis=-1, keepdims=True)
        outs.append((jnp.dot(p.astype(BF16), vh, preferred_element_type=F32) / l).astype(BF16))
    o = jnp.concatenate(outs, axis=1)
    y = jnp.dot(o, wo_ref[...], preferred_element_type=F32)
    o_ref[...] = x + _rms(y, gpost_ref[...])


def memory_xattn(x, gpre, wq, memkv, mem_block, wo, gpost, tm):
    t, d = x.shape
    full = lambda a: pl.BlockSpec(a.shape, lambda i: (0,) * a.ndim)
    kern = functools.partial(_xattn_kernel, q_scale=HEAD_DIM ** -0.5 * LOG2E)
    return pl.pallas_call(
        kern,
        grid=(t // tm,),
        in_specs=[
            pl.BlockSpec((tm, d), lambda i: (i, 0)),
            full(gpre), full(wq),
            pl.BlockSpec((1,) + memkv.shape[1:], lambda i: (mem_block(i), 0, 0)),
            full(wo), full(gpost),
        ],
        out_specs=pl.BlockSpec((tm, d), lambda i: (i, 0)),
        out_shape=jax.ShapeDtypeStruct((t, d), F32),
        compiler_params=_cparams(("parallel",)),
        name="memory_xattn",
    )(x, gpre, wq, memkv, wo, gpost)


def _ffn_kernel(x_ref, gpre_ref, wg_ref, wu_ref, wd_ref, gpost_ref, o_ref, h_ref, acc_ref):
    f = pl.program_id(1)

    @pl.when(f == 0)
    def _():
        h_ref[...] = _rms(x_ref[...], gpre_ref[...]).astype(BF16)
        acc_ref[...] = jnp.zeros_like(acc_ref)

    h = h_ref[...]
    a = jnp.dot(h, wg_ref[...], preferred_element_type=F32)
    b = jnp.dot(h, wu_ref[...], preferred_element_type=F32)
    t = (a * jax.nn.sigmoid(a) * b).astype(BF16)
    acc_ref[...] += jnp.dot(t, wd_ref[...], preferred_element_type=F32)

    @pl.when(f == pl.num_programs(1) - 1)
    def _():
        o_ref[...] = x_ref[...] + _rms(acc_ref[...], gpost_ref[...])


def swiglu_ffn(x, gpre, wg, wu, wd, layer, gpost, tm, tf):
    t, d = x.shape
    ff = wg.shape[2]
    return pl.pallas_call(
        _ffn_kernel,
        grid=(t // tm, ff // tf),
        in_specs=[
            pl.BlockSpec((tm, d), lambda i, f: (i, 0)),
            pl.BlockSpec((1, d), lambda i, f: (0, 0)),
            pl.BlockSpec((None, d, tf), lambda i, f: (layer, 0, f)),
            pl.BlockSpec((None, d, tf), lambda i, f: (layer, 0, f)),
            pl.BlockSpec((None, tf, d), lambda i, f: (layer, f, 0)),
            pl.BlockSpec((1, d), lambda i, f: (0, 0)),
        ],
        out_specs=pl.BlockSpec((tm, d), lambda i, f: (i, 0)),
        out_shape=jax.ShapeDtypeStruct((t, d), F32),
        scratch_shapes=[pltpu.VMEM((tm, d), BF16), pltpu.VMEM((tm, d), F32)],
        compiler_params=_cparams(("parallel", "arbitrary")),
        name="swiglu_ffn",
    )(x, gpre, wg, wu, wd, gpost)


def _rotary_tables(seq):
    pos = jnp.arange(seq)

    def cs(p):
        inv = ROPE_THETA ** (-jnp.arange(0, 64, 2, dtype=F32) / 64)
        ang = p.astype(F32)[:, None] * inv[None, :]
        c, s = jnp.cos(ang), jnp.sin(ang)
        return jnp.concatenate([c, c], axis=-1), jnp.concatenate([-s, s], axis=-1)

    c_row, s_row = cs(pos // GRID_W)
    c_col, s_col = cs(pos % GRID_W)
    c_tok, s_tok = cs(pos)
    return (jnp.concatenate([c_row, c_col], axis=-1), jnp.concatenate([s_row, s_col], axis=-1),
            jnp.concatenate([c_tok, c_tok], axis=-1), jnp.concatenate([s_tok, s_tok], axis=-1))


def _t5_bucket(rel):
    nb = NUM_BUCKETS // 2
    max_exact = nb // 2
    ret = (rel > 0).astype(jnp.int32) * nb
    n = jnp.abs(rel)
    nf = jnp.maximum(n, 1).astype(F32)
    large = max_exact + (jnp.log(nf / max_exact) / math.log(MAX_DISTANCE / max_exact)
                         * (nb - max_exact)).astype(jnp.int32)
    large = jnp.minimum(large, nb - 1)
    return ret + jnp.where(n < max_exact, n, large)


T5_FAR = 91


def _bias_kernel(tab_ref, o_ref, *, tq, tk, r):
    h = pl.program_id(0)
    c_left = tab_ref[h, NUM_BUCKETS // 2 - 1] * LOG2E
    c_right = tab_ref[h, NUM_BUCKETS - 1] * LOG2E
    for t in range(r + 4):
        off = (t - (r + 1)) * tq
        lo = min(max((-off - T5_FAR + 1) // 128 * 128, 0), tk)
        hi = min(max(-((off - tq - T5_FAR + 1) // 128) * 128, 0), tk)
        if lo > 0:
            o_ref[0, t, :, :lo] = jnp.full((tq, lo), c_left, F32)
        if hi < tk:
            o_ref[0, t, :, hi:] = jnp.full((tq, tk - hi), c_right, F32)
        if hi > lo:
            rel = (off + lo + lax.broadcasted_iota(jnp.int32, (tq, hi - lo), 1)
                   - lax.broadcasted_iota(jnp.int32, (tq, hi - lo), 0))
            bucket = _t5_bucket(rel)
            v = jnp.full((tq, hi - lo), tab_ref[h, 0], F32)
            for b in range(1, NUM_BUCKETS):
                v = jnp.where(bucket == b, tab_ref[h, b], v)
            o_ref[0, t, :, lo:hi] = v * LOG2E


def _bias_tiles(rel_bias, tq, tk):
    assert tq >= 128 and tk % tq == 0
    n = np.arange(T5_FAR, 1 << 16, dtype=np.float32)
    far_bucket = np.minimum(8 + (np.log(n / 8) / math.log(MAX_DISTANCE / 8) * 8).astype(np.int32), 15)
    assert NUM_BUCKETS == 32 and far_bucket.min() == 15
    r = tk // tq
    return pl.pallas_call(
        functools.partial(_bias_kernel, tq=tq, tk=tk, r=r),
        grid=(HEADS,),
        in_specs=[pl.BlockSpec(memory_space=pltpu.SMEM)],
        out_specs=pl.BlockSpec((1, r + 4, tq, tk), lambda h: (h, 0, 0, 0)),
        out_shape=jax.ShapeDtypeStruct((HEADS, r + 4, tq, tk), F32),
        compiler_params=_cparams(("parallel",)),
        name="bias_tiles",
    )(rel_bias.T.astype(F32))


def kernel(x_prompt, x_sample, mem_prompt, mem_sample, rel_bias, g_mix_pre, g_mix_post, w_in, lam_q1, lam_k1, lam_q2, lam_k2, g_a_out, g_b_q, g_b_k, g_c_q, g_c_kv, w_c_q_up, w_c_kv_up, w_br_a, w_br_b, w_br_c, w_mix_out, g_x_pre, g_x_post, g_mem, w_x_q, w_x_kv, w_x_out, g_ffn_pre, g_ffn_post, w_ffn_gate, w_ffn_up, w_ffn_down):
    depth = w_in.shape[0]
    d = x_prompt.shape[-1]
    tm_row = 512
    tq_a, tk_a = 256, 1024
    tq, tk = 512, 1024

    groups = [(x.reshape(-1, d), m, x.shape[0], x.shape[1])
              for x, m in ((x_prompt, mem_prompt), (x_sample, mem_sample))]
    tables = _rotary_tables(max(g[3] for g in groups))
    bias = _bias_tiles(rel_bias, tq_a, tk_a)
    row = lambda g: g.reshape(1, -1).astype(F32)
    in_scale = jnp.concatenate([jnp.full((1, Z_AK), A_QK_DIM ** -0.5 * LOG2E, F32),
                                jnp.ones((1, Z_COLS - Z_AK), F32)], axis=-1)
    ones_kv = jnp.ones((1, w_x_kv.shape[-1]), F32)
    kv_group = HEADS // B_KV_HEADS

    w_in_p = w_in_prep(jnp.swapaxes(w_in, 1, 2), tr=2048, tn=512)
    w_xkv_all = w_x_kv.astype(BF16)
    w_fg, w_fu, w_fd = w_ffn_gate.astype(BF16), w_ffn_up.astype(BF16), w_ffn_down.astype(BF16)

    xs = [g[0] for g in groups]
    for l in range(depth):
        lam_init = 0.8 - 0.6 * math.exp(-0.3 * l)
        lam = (jnp.exp(jnp.sum(lam_q1[l].astype(F32) * lam_k1[l].astype(F32)))
               - jnp.exp(jnp.sum(lam_q2[l].astype(F32) * lam_k2[l].astype(F32))) + lam_init).reshape(1)

        wq3 = w_c_q_up[l].reshape(C_Q_RANK, HEADS, C_NOPE + C_ROPE)
        w_cq = jnp.concatenate(
            [wq3[..., :C_NOPE].reshape(C_Q_RANK, HEADS * C_NOPE),
             jnp.pad(wq3[..., C_NOPE:], ((0, 0), (0, 0), (0, HEAD_DIM - C_ROPE))).reshape(
                 C_Q_RANK, HEADS * HEAD_DIM)], axis=-1).astype(BF16)
        w_ckv = w_c_kv_up[l].astype(BF16)
        w_a, w_b, w_c = w_br_a[l].astype(BF16), w_br_b[l].astype(BF16), w_br_c[l].astype(BF16)
        w_mix = w_mix_out[l].astype(BF16)
        w_xq, w_xo = w_x_q[l].astype(BF16), w_x_out[l].astype(BF16)

        for gi, (_, mem, nseq, seq) in enumerate(groups):
            x = xs[gi]
            blocks_per_seq = seq // tm_row
            streams = 2 if seq // tk <= 4 else 1
            z = rms_matmul(x, row(g_mix_pre[l]), w_in_p, l, in_scale, tm=1024, tn=512)
            qb, kb, cq, ckv, kr = mixer_prep(z, tables, lambda i: i % blocks_per_seq, row(g_b_q[l]), row(g_b_k[l]),
                                             row(g_c_q[l]), row(g_c_kv[l]), w_cq, w_ckv, tm_row)
            oa = attention_a(lam, z, bias, row(g_a_out[l]), nseq, seq, tq_a, tk_a, 1.0 - lam_init, streams)
            ob = attention([(qb, lambda h: h)],
                           [(kb, lambda h: h // kv_group)],
                           (z, lambda h: Z_BV // HEAD_DIM + h // kv_group),
                           nseq, seq, tq, tk, streams)
            oc = attention([(cq, lambda h: h), (cq, lambda h: HEADS + h)],
                           [(ckv, lambda h: 2 * h), (kr, lambda h: 0)],
                           (ckv, lambda h: 2 * h + 1),
                           nseq, seq, tq, tk, streams)
            merged = gated_merge(oa, ob, oc, w_a, w_b, w_c, z, tm=1024, tn=512)
            x = mm_postnorm_residual(merged, w_mix, row(g_mix_post[l]), x, tm=tm_row)

            mem2 = mem.reshape(nseq * MEM_TOKENS, d)
            memkv = rms_matmul(mem2, row(g_mem[l]), w_xkv_all, l, ones_kv, tm=mem2.shape[0], tn=512)
            memkv = memkv.reshape(nseq, MEM_TOKENS, -1)
            x = memory_xattn(x, row(g_x_pre[l]), w_xq, memkv, lambda i: i // blocks_per_seq, w_xo,
                             row(g_x_post[l]), tm_row)
            xs[gi] = swiglu_ffn(x, row(g_ffn_pre[l]), w_fg, w_fu, w_fd, l, row(g_ffn_post[l]), tm=tm_row, tf=512)

    return tuple(x.reshape(g[2], g[3], d) for x, g in zip(xs, groups))
```

```python
import functools
import math

import jax
import jax.numpy as jnp
import numpy as np
from jax import lax
from jax.experimental import pallas as pl
from jax.experimental.pallas import tpu as pltpu

F32 = jnp.float32
BF16 = jnp.bfloat16

EPS = 1e-6
ROPE_THETA = 10000.0
LOG2E = 1.4426950408889634
GRID_W = 64
NUM_BUCKETS = 32
MAX_DISTANCE = 128

D_MODEL = 2048
HEADS = 8
HEAD_DIM = 128
A_QK_DIM = 64
B_KV_HEADS = 2
C_Q_RANK = 512
C_KV_RANK = 256
C_NOPE = 128
C_ROPE = 64
X_HEADS = 4
MEM_TOKENS = 256

Z_AQ, Z_AK, Z_AV, Z_BQ, Z_BK, Z_BV, Z_CQA, Z_CKVA, Z_CKR = 0, 1024, 2048, 3072, 4096, 4352, 4608, 5120, 5376
Z_RAW_SPLIT = 5440
Z_GATES = 5632
Z_COLS = Z_GATES + 3 * D_MODEL

VMEM_LIMIT = 56 * 1024 * 1024


def _cparams(sem):
    return pltpu.CompilerParams(dimension_semantics=sem, vmem_limit_bytes=VMEM_LIMIT)


def _rms(x, g):
    return x * lax.rsqrt(jnp.mean(x * x, axis=-1, keepdims=True) + EPS) * g


def _swap32(x):
    lane = lax.broadcasted_iota(jnp.int32, x.shape, 1)
    return jnp.where((lane % 64) < 32, pltpu.roll(x, 96, 1), pltpu.roll(x, 32, 1))


def _rope(x, cos, sin_signed):
    return x * cos + _swap32(x) * sin_signed


def _rms_matmul_kernel(x_ref, g_ref, w_ref, cs_ref, o_ref, xn_ref):
    @pl.when(pl.program_id(1) == 0)
    def _():
        xn_ref[...] = _rms(x_ref[...].astype(F32), g_ref[...]).astype(BF16)

    acc = jnp.dot(xn_ref[...], w_ref[...], preferred_element_type=F32)
    o_ref[...] = (acc * cs_ref[...]).astype(o_ref.dtype)


def rms_matmul(x, g, w, layer, colscale, tm, tn, out_dtype=BF16):
    m, k = x.shape
    n = w.shape[2]
    return pl.pallas_call(
        _rms_matmul_kernel,
        grid=(m // tm, n // tn),
        in_specs=[
            pl.BlockSpec((tm, k), lambda i, j: (i, 0)),
            pl.BlockSpec((1, k), lambda i, j: (0, 0)),
            pl.BlockSpec((None, k, tn), lambda i, j: (layer, 0, j)),
            pl.BlockSpec((1, tn), lambda i, j: (0, j)),
        ],
        out_specs=pl.BlockSpec((tm, tn), lambda i, j: (i, j)),
        out_shape=jax.ShapeDtypeStruct((m, n), out_dtype),
        scratch_shapes=[pltpu.VMEM((tm, k), BF16)],
        compiler_params=_cparams(("parallel", "arbitrary")),
        name="rms_matmul",
    )(x, g, w, colscale)


def _w_in_prep_kernel(lo_ref, hi_ref, o_ref, *, split_block, split_row):
    j = pl.program_id(2)

    def emit(rows):
        o_ref[...] = rows.T.astype(BF16)

    @pl.when(j < split_block)
    def _():
        emit(hi_ref[...])

    @pl.when(j == split_block)
    def _():
        row = lax.broadcasted_iota(jnp.int32, hi_ref.shape, 0)
        emit(jnp.where(row < split_row, hi_ref[...], 0.0))

    @pl.when(j > split_block)
    def _():
        emit(jnp.concatenate([lo_ref[split_row:, :], hi_ref[:split_row, :]], axis=0))


def w_in_prep(w_in_t, tr, tn):
    depth, raw, d = w_in_t.shape
    pad = Z_GATES - Z_RAW_SPLIT
    assert Z_COLS == raw + pad and pad < tn
    split_block, split_row = divmod(Z_RAW_SPLIT, tn)
    assert split_row % 8 == 0
    last = (raw - 1) // tn
    return pl.pallas_call(
        functools.partial(_w_in_prep_kernel, split_block=split_block, split_row=split_row),
        grid=(depth, d // tr, Z_COLS // tn),
        in_specs=[
            pl.BlockSpec((None, tn, tr), lambda l, i, j: (l, jnp.where(j > split_block, j - 1, 0), i)),
            pl.BlockSpec((None, tn, tr), lambda l, i, j: (l, jnp.minimum(j, last), i)),
        ],
        out_specs=pl.BlockSpec((None, tr, tn), lambda l, i, j: (l, i, j)),
        out_shape=jax.ShapeDtypeStruct((depth, d, Z_COLS), BF16),
        compiler_params=_cparams(("parallel", "parallel", "arbitrary")),
        name="w_in_prep",
    )(w_in_t, w_in_t)


def _prep_kernel(bq_ref, bkv_ref, cqa_ref, ckk_ref, cos_a_ref, sin_a_ref, cos_r_ref, sin_r_ref,
                 gbq_ref, gbk_ref, gcq_ref, gckv_ref, wq_ref, wkv_ref,
                 qb_out, kb_out, cq_out, ckv_out, kr_out, *, b_scale, c_scale):
    cos_a, sin_a = cos_a_ref[...], sin_a_ref[...]
    cos_r, sin_r = cos_r_ref[...], sin_r_ref[...]

    for h in range(HEADS):
        sl = slice(h * HEAD_DIM, (h + 1) * HEAD_DIM)
        y = _rms(bq_ref[:, sl].astype(F32), gbq_ref[...])
        qb_out[:, sl] = (_rope(y, cos_a, sin_a) * b_scale).astype(BF16)
    for n in range(B_KV_HEADS):
        sl = slice(n * HEAD_DIM, (n + 1) * HEAD_DIM)
        y = _rms(bkv_ref[:, sl].astype(F32), gbk_ref[...])
        kb_out[:, sl] = _rope(y, cos_a, sin_a).astype(BF16)

    xq = _rms(cqa_ref[...].astype(F32), gcq_ref[...]).astype(BF16)
    cq = jnp.dot(xq, wq_ref[...], preferred_element_type=F32)
    nope_cols = HEADS * C_NOPE
    cq_out[:, :nope_cols] = (cq[:, :nope_cols] * c_scale).astype(BF16)
    for h in range(HEADS):
        sl = slice(nope_cols + h * HEAD_DIM, nope_cols + (h + 1) * HEAD_DIM)
        cq_out[:, sl] = (_rope(cq[:, sl], cos_r, sin_r) * c_scale).astype(BF16)

    xkv = _rms(ckk_ref[:, :C_KV_RANK].astype(F32), gckv_ref[...]).astype(BF16)
    ckv_out[...] = jnp.dot(xkv, wkv_ref[...], preferred_element_type=F32).astype(BF16)
    kr = ckk_ref[:, C_KV_RANK:C_KV_RANK + HEAD_DIM].astype(F32)
    kr_out[...] = _rope(kr, cos_r, sin_r).astype(BF16)


def mixer_prep(z, tables, pos_block, gbq, gbk, gcq, gckv, wq, wkv, tm):
    t = z.shape[0]
    cos_a, sin_a, cos_r, sin_r = tables
    row = lambda w: pl.BlockSpec((tm, w), lambda i: (i, 0))
    tab = pl.BlockSpec((tm, HEAD_DIM), lambda i: (pos_block(i), 0))
    full = lambda a: pl.BlockSpec(a.shape, lambda i: (0,) * a.ndim)
    kern = functools.partial(_prep_kernel, b_scale=HEAD_DIM ** -0.5 * LOG2E,
                             c_scale=(C_NOPE + C_ROPE) ** -0.5 * LOG2E)
    return pl.pallas_call(
        kern,
        grid=(t // tm,),
        in_specs=[
            pl.BlockSpec((tm, 1024), lambda i: (i, Z_BQ // 1024)),
            pl.BlockSpec((tm, 512), lambda i: (i, Z_BK // 512)),
            pl.BlockSpec((tm, 512), lambda i: (i, Z_CQA // 512)),
            pl.BlockSpec((tm, 512), lambda i: (i, Z_CKVA // 512)),
            tab, tab, tab, tab,
            full(gbq), full(gbk), full(gcq), full(gckv), full(wq), full(wkv),
        ],
        out_specs=[row(1024), row(256), row(2048), row(2048), row(HEAD_DIM)],
        out_shape=[
            jax.ShapeDtypeStruct((t, 1024), BF16),
            jax.ShapeDtypeStruct((t, 256), BF16),
            jax.ShapeDtypeStruct((t, 2048), BF16),
            jax.ShapeDtypeStruct((t, 2048), BF16),
            jax.ShapeDtypeStruct((t, HEAD_DIM), BF16),
        ],
        compiler_params=_cparams(("parallel",)),
        name="mixer_prep",
    )(z, z, z, z, cos_a, sin_a, cos_r, sin_r, gbq, gbk, gcq, gckv, wq, wkv)


def _flash(q, k_refs, v_ref, s_refs, nk, tk, chunk_fn=None, bias_fn=None):
    m_rows = q.shape[0]
    dv = v_ref.shape[1]
    ones = jnp.ones((tk, dv), BF16)

    def rows_of(ref, chunk):
        if isinstance(chunk, int):
            return ref[chunk * tk:(chunk + 1) * tk, :]
        return ref[pl.ds(pl.multiple_of(chunk * tk, tk), tk), :]

    def scores(chunk):
        ks = [rows_of(r, chunk) for r in k_refs]
        k = ks[0] if len(ks) == 1 else jnp.concatenate(ks, axis=1)
        return lax.dot_general(q, k, (((1,), (1,)), ((), ())), preferred_element_type=F32)

    def update(pos, chunk, s, carry):
        m, acc = carry
        c = None
        if bias_fn is not None:
            s, c = bias_fn(pos, chunk, s)
        row_max = jnp.max(s, axis=-1, keepdims=True)
        if c is None:
            m_new = jnp.maximum(m, row_max)
            shift = m_new
        else:
            m_new = jnp.maximum(m, row_max + c)
            shift = m_new - c
        p = jnp.exp2((s - shift).astype(BF16))
        alpha = jnp.exp2(m - m_new)
        v_aug = jnp.concatenate([rows_of(v_ref, chunk), ones], axis=1)
        pv = jnp.dot(p, v_aug, preferred_element_type=F32)
        return m_new, alpha * acc + pv

    chunks = [pos if chunk_fn is None else chunk_fn(pos) for pos in range(nk)]
    carry = (jnp.full((m_rows, 1), -jnp.inf, F32), jnp.zeros((m_rows, 2 * dv), F32))
    s_refs[0][...] = scores(chunks[0])
    for pos in range(nk):
        if pos + 1 < nk:
            s_refs[(pos + 1) % 2][...] = scores(chunks[pos + 1])
        carry = update(pos, chunks[pos], s_refs[pos % 2][...], carry)
    m, acc = carry
    return m, acc[:, dv:], acc[:, :dv]


def _attn_kernel(*refs, n_q, n_k, nk, tk, streams):
    n_s = 2 * streams
    q_refs, k_refs = refs[:n_q], refs[n_q:n_q + n_k]
    v_ref, o_ref, s_refs = refs[n_q + n_k], refs[-n_s - 1], refs[-n_s:]
    rows = o_ref.shape[0] // streams
    for t in range(streams):
        sl = slice(t * rows, (t + 1) * rows)
        qs = [r[sl, :] for r in q_refs]
        q = qs[0] if n_q == 1 else jnp.concatenate(qs, axis=1)
        _, l, acc = _flash(q, k_refs, v_ref, s_refs[2 * t:2 * t + 2], nk, tk)
        o_ref[sl, :] = (acc / l).astype(o_ref.dtype)


def _attn_a_kernel(lam_ref, far_ref, q_ref, k_ref, v_ref, bias_ref, g_ref, *rest, tq, tk, nk, r, post_scale,
                   streams):
    n_s = 2 * streams
    o_ref, s_refs = rest[-n_s - 1], rest[-n_s:]
    h = pl.program_id(1)

    for t in range(streams):
        qi = pl.program_id(2) * streams + t
        diag = qi // r
        q = q_ref[t * tq:(t + 1) * tq, :]
        lane = lax.broadcasted_iota(jnp.int32, q.shape, 1)
        zero = jnp.zeros_like(q)
        q2 = jnp.concatenate([jnp.where(lane < A_QK_DIM, q, zero), jnp.where(lane >= A_QK_DIM, q, zero)], axis=0)

        def chunk_fn(pos, diag=diag):
            return (diag + pos) % nk

        def bias_fn(pos, chunk, s, qi=qi, diag=diag):
            tile = jnp.clip(chunk * r - qi, -r - 1, 2) + r + 1
            if pos == 0:
                b = bias_ref[0, tile]
                return jnp.concatenate([s[:tq] + b, s[tq:] + b], axis=0), None
            c = jnp.where(chunk > diag, far_ref[h, 1], far_ref[h, 0])
            edges = ([slice(0, 128)] if pos == 1 else []) + ([slice(tk - 128, tk)] if pos == nk - 1 else [])
            for cols in edges:
                d = bias_ref[0, tile, :, cols] - c
                parts = [s[:, :cols.start], s[:, cols] + jnp.concatenate([d, d], axis=0), s[:, cols.stop:]]
                s = jnp.concatenate([x for x in parts if x.shape[1]], axis=1)
            return s, c

        _, l, acc = _flash(q2, [k_ref], v_ref, s_refs[2 * t:2 * t + 2], nk, tk, chunk_fn, bias_fn)
        o = acc / l
        d = o[:tq] - lam_ref[0] * o[tq:]
        o_ref[t * tq:(t + 1) * tq, :] = (_rms(d, g_ref[...]) * post_scale).astype(o_ref.dtype)


def attention(q_parts, k_parts, v_part, nseq, seq, tq, tk, streams):
    rows = tq * streams
    nq = seq // rows
    in_specs, args = [], []
    for a, cf in q_parts:
        in_specs.append(pl.BlockSpec((rows, HEAD_DIM), lambda s, h, i, cf=cf: (s * nq + i, cf(h))))
        args.append(a)
    for a, cf in list(k_parts) + [v_part]:
        in_specs.append(pl.BlockSpec((seq, HEAD_DIM), lambda s, h, i, cf=cf: (s, cf(h))))
        args.append(a)
    kern = functools.partial(_attn_kernel, n_q=len(q_parts), n_k=len(k_parts), nk=seq // tk, tk=tk,
                             streams=streams)
    return pl.pallas_call(
        kern,
        grid=(nseq, HEADS, nq),
        in_specs=in_specs,
        out_specs=pl.BlockSpec((rows, HEAD_DIM), lambda s, h, i: (s * nq + i, h)),
        out_shape=jax.ShapeDtypeStruct((nseq * seq, HEADS * HEAD_DIM), BF16),
        scratch_shapes=[pltpu.VMEM((tq, tk), F32)] * (2 * streams),
        compiler_params=_cparams(("parallel", "parallel", "arbitrary")),
        name="attention",
    )(*args)


def attention_a(lam, z, bias, g, nseq, seq, tq, tk, post_scale, streams):
    rows = tq * streams
    nq = seq // rows
    r = tk // tq
    far = bias[:, :: r + 3, 0, 0]
    in_specs = [
        pl.BlockSpec(memory_space=pltpu.SMEM),
        pl.BlockSpec(memory_space=pltpu.SMEM),
        pl.BlockSpec((rows, HEAD_DIM), lambda s, h, i: (s * nq + i, Z_AQ // HEAD_DIM + h)),
        pl.BlockSpec((seq, HEAD_DIM), lambda s, h, i: (s, Z_AK // HEAD_DIM + h)),
        pl.BlockSpec((seq, HEAD_DIM), lambda s, h, i: (s, Z_AV // HEAD_DIM + h)),
        pl.BlockSpec((1, r + 4, tq, tk), lambda s, h, i: (h, 0, 0, 0)),
        pl.BlockSpec((1, HEAD_DIM), lambda s, h, i: (0, 0)),
    ]
    kern = functools.partial(_attn_a_kernel, tq=tq, tk=tk, nk=seq // tk, r=r, post_scale=post_scale,
                             streams=streams)
    return pl.pallas_call(
        kern,
        grid=(nseq, HEADS, nq),
        in_specs=in_specs,
        out_specs=pl.BlockSpec((rows, HEAD_DIM), lambda s, h, i: (s * nq + i, h)),
        out_shape=jax.ShapeDtypeStruct((nseq * seq, HEADS * HEAD_DIM), BF16),
        scratch_shapes=[pltpu.VMEM((2 * tq, tk), F32)] * (2 * streams),
        compiler_params=_cparams(("parallel", "parallel", "arbitrary")),
        name="attention_a",
    )(lam, far, z, z, z, bias, g)


def _merge_kernel(oa_ref, ob_ref, oc_ref, wa_ref, wb_ref, wc_ref, ga_ref, gb_ref, gc_ref, o_ref):
    def branch(o_r, w_r, g_r):
        y = jnp.dot(o_r[...], w_r[...], preferred_element_type=F32)
        return jax.nn.sigmoid(g_r[...].astype(F32)) * y

    merged = branch(oa_ref, wa_ref, ga_ref) + branch(ob_ref, wb_ref, gb_ref) + branch(oc_ref, wc_ref, gc_ref)
    o_ref[...] = merged.astype(o_ref.dtype)


def gated_merge(oa, ob, oc, wa, wb, wc, z, tm, tn):
    t, k = oa.shape
    n = wa.shape[1]
    o_spec = pl.BlockSpec((tm, k), lambda i, j: (i, 0))
    w_spec = pl.BlockSpec((k, tn), lambda i, j: (0, j))
    gate = lambda b: pl.BlockSpec((tm, tn), lambda i, j: (i, (Z_GATES + b * n) // tn + j))
    return pl.pallas_call(
        _merge_kernel,
        grid=(t // tm, n // tn),
        in_specs=[o_spec, o_spec, o_spec, w_spec, w_spec, w_spec, gate(0), gate(1), gate(2)],
        out_specs=pl.BlockSpec((tm, tn), lambda i, j: (i, j)),
        out_shape=jax.ShapeDtypeStruct((t, n), BF16),
        compiler_params=_cparams(("parallel", "arbitrary")),
        name="gated_merge",
    )(oa, ob, oc, wa, wb, wc, z, z, z)


def _mm_postnorm_kernel(a_ref, w_ref, g_ref, x_ref, o_ref):
    y = jnp.dot(a_ref[...], w_ref[...], preferred_element_type=F32)
    o_ref[...] = x_ref[...] + _rms(y, g_ref[...])


def mm_postnorm_residual(a, w, g, x, tm):
    t, k = a.shape
    n = w.shape[1]
    return pl.pallas_call(
        _mm_postnorm_kernel,
        grid=(t // tm,),
        in_specs=[
            pl.BlockSpec((tm, k), lambda i: (i, 0)),
            pl.BlockSpec((k, n), lambda i: (0, 0)),
            pl.BlockSpec((1, n), lambda i: (0, 0)),
            pl.BlockSpec((tm, n), lambda i: (i, 0)),
        ],
        out_specs=pl.BlockSpec((tm, n), lambda i: (i, 0)),
        out_shape=jax.ShapeDtypeStruct((t, n), F32),
        compiler_params=_cparams(("parallel",)),
        name="mm_postnorm_residual",
    )(a, w, g, x)


def _xattn_kernel(x_ref, gpre_ref, wq_ref, kv_ref, wo_ref, gpost_ref, o_ref, *, q_scale):
    x = x_ref[...]
    h = _rms(x, gpre_ref[...]).astype(BF16)
    q = (jnp.dot(h, wq_ref[...], preferred_element_type=F32) * q_scale).astype(BF16)
    kv_cols = X_HEADS * HEAD_DIM
    outs = []
    for hd in range(X_HEADS):
        sl = slice(hd * HEAD_DIM, (hd + 1) * HEAD_DIM)
        kh = kv_ref[0, :, sl]
        vh = kv_ref[0, :, kv_cols + hd * HEAD_DIM:kv_cols + (hd + 1) * HEAD_DIM]
        s = lax.dot_general(q[:, sl], kh, (((1,), (1,)), ((), ())), preferred_element_type=F32)
        p = jnp.exp2(s - jnp.max(s, axis=-1, keepdims=True))
        l = jnp.sum(p, axis=-1, keepdims=True)
        outs.append((jnp.dot(p.astype(BF16), vh, preferred_element_type=F32) / l).astype(BF16))
    o = jnp.concatenate(outs, axis=1)
    y = jnp.dot(o, wo_ref[...], preferred_element_type=F32)
    o_ref[...] = x + _rms(y, gpost_ref[...])


def memory_xattn(x, gpre, wq, memkv, mem_block, wo, gpost, tm):
    t, d = x.shape
    full = lambda a: pl.BlockSpec(a.shape, lambda i: (0,) * a.ndim)
    kern = functools.partial(_xattn_kernel, q_scale=HEAD_DIM ** -0.5 * LOG2E)
    return pl.pallas_call(
        kern,
        grid=(t // tm,),
        in_specs=[
            pl.BlockSpec((tm, d), lambda i: (i, 0)),
            full(gpre), full(wq),
            pl.BlockSpec((1,) + memkv.shape[1:], lambda i: (mem_block(i), 0, 0)),
            full(wo), full(gpost),
        ],
        out_specs=pl.BlockSpec((tm, d), lambda i: (i, 0)),
        out_shape=jax.ShapeDtypeStruct((t, d), F32),
        compiler_params=_cparams(("parallel",)),
        name="memory_xattn",
    )(x, gpre, wq, memkv, wo, gpost)


def _ffn_kernel(x_ref, gpre_ref, wg_ref, wu_ref, wd_ref, gpost_ref, o_ref, h_ref, acc_ref):
    f = pl.program_id(1)

    @pl.when(f == 0)
    def _():
        h_ref[...] = _rms(x_ref[...], gpre_ref[...]).astype(BF16)
        acc_ref[...] = jnp.zeros_like(acc_ref)

    h = h_ref[...]
    a = jnp.dot(h, wg_ref[...], preferred_element_type=F32)
    b = jnp.dot(h, wu_ref[...], preferred_element_type=F32)
    t = (a * jax.nn.sigmoid(a) * b).astype(BF16)
    acc_ref[...] += jnp.dot(t, wd_ref[...], preferred_element_type=F32)

    @pl.when(f == pl.num_programs(1) - 1)
    def _():
        o_ref[...] = x_ref[...] + _rms(acc_ref[...], gpost_ref[...])


def swiglu_ffn(x, gpre, wg, wu, wd, layer, gpost, tm, tf):
    t, d = x.shape
    ff = wg.shape[2]
    return pl.pallas_call(
        _ffn_kernel,
        grid=(t // tm, ff // tf),
        in_specs=[
            pl.BlockSpec((tm, d), lambda i, f: (i, 0)),
            pl.BlockSpec((1, d), lambda i, f: (0, 0)),
            pl.BlockSpec((None, d, tf), lambda i, f: (layer, 0, f)),
            pl.BlockSpec((None, d, tf), lambda i, f: (layer, 0, f)),
            pl.BlockSpec((None, tf, d), lambda i, f: (layer, f, 0)),
            pl.BlockSpec((1, d), lambda i, f: (0, 0)),
        ],
        out_specs=pl.BlockSpec((tm, d), lambda i, f: (i, 0)),
        out_shape=jax.ShapeDtypeStruct((t, d), F32),
        scratch_shapes=[pltpu.VMEM((tm, d), BF16), pltpu.VMEM((tm, d), F32)],
        compiler_params=_cparams(("parallel", "arbitrary")),
        name="swiglu_ffn",
    )(x, gpre, wg, wu, wd, gpost)


def _rotary_tables(seq):
    pos = jnp.arange(seq)

    def cs(p):
        inv = ROPE_THETA ** (-jnp.arange(0, 64, 2, dtype=F32) / 64)
        ang = p.astype(F32)[:, None] * inv[None, :]
        c, s = jnp.cos(ang), jnp.sin(ang)
        return jnp.concatenate([c, c], axis=-1), jnp.concatenate([-s, s], axis=-1)

    c_row, s_row = cs(pos // GRID_W)
    c_col, s_col = cs(pos % GRID_W)
    c_tok, s_tok = cs(pos)
    return (jnp.concatenate([c_row, c_col], axis=-1), jnp.concatenate([s_row, s_col], axis=-1),
            jnp.concatenate([c_tok, c_tok], axis=-1), jnp.concatenate([s_tok, s_tok], axis=-1))


def _t5_bucket(rel):
    nb = NUM_BUCKETS // 2
    max_exact = nb // 2
    ret = (rel > 0).astype(jnp.int32) * nb
    n = jnp.abs(rel)
    nf = jnp.maximum(n, 1).astype(F32)
    large = max_exact + (jnp.log(nf / max_exact) / math.log(MAX_DISTANCE / max_exact)
                         * (nb - max_exact)).astype(jnp.int32)
    large = jnp.minimum(large, nb - 1)
    return ret + jnp.where(n < max_exact, n, large)


T5_FAR = 91


def _bias_kernel(tab_ref, o_ref, *, tq, tk, r):
    h = pl.program_id(0)
    c_left = tab_ref[h, NUM_BUCKETS // 2 - 1] * LOG2E
    c_right = tab_ref[h, NUM_BUCKETS - 1] * LOG2E
    for t in range(r + 4):
        off = (t - (r + 1)) * tq
        lo = min(max((-off - T5_FAR + 1) // 128 * 128, 0), tk)
        hi = min(max(-((off - tq - T5_FAR + 1) // 128) * 128, 0), tk)
        if lo > 0:
            o_ref[0, t, :, :lo] = jnp.full((tq, lo), c_left, F32)
        if hi < tk:
            o_ref[0, t, :, hi:] = jnp.full((tq, tk - hi), c_right, F32)
        if hi > lo:
            rel = (off + lo + lax.broadcasted_iota(jnp.int32, (tq, hi - lo), 1)
                   - lax.broadcasted_iota(jnp.int32, (tq, hi - lo), 0))
            bucket = _t5_bucket(rel)
            v = jnp.full((tq, hi - lo), tab_ref[h, 0], F32)
            for b in range(1, NUM_BUCKETS):
                v = jnp.where(bucket == b, tab_ref[h, b], v)
            o_ref[0, t, :, lo:hi] = v * LOG2E


def _bias_tiles(rel_bias, tq, tk):
    assert tq >= 128 and tk % tq == 0
    n = np.arange(T5_FAR, 1 << 16, dtype=np.float32)
    far_bucket = np.minimum(8 + (np.log(n / 8) / math.log(MAX_DISTANCE / 8) * 8).astype(np.int32), 15)
    assert NUM_BUCKETS == 32 and far_bucket.min() == 15
    r = tk // tq
    return pl.pallas_call(
        functools.partial(_bias_kernel, tq=tq, tk=tk, r=r),
        grid=(HEADS,),
        in_specs=[pl.BlockSpec(memory_space=pltpu.SMEM)],
        out_specs=pl.BlockSpec((1, r + 4, tq, tk), lambda h: (h, 0, 0, 0)),
        out_shape=jax.ShapeDtypeStruct((HEADS, r + 4, tq, tk), F32),
        compiler_params=_cparams(("parallel",)),
        name="bias_tiles",
    )(rel_bias.T.astype(F32))


def kernel(x_prompt, x_sample, mem_prompt, mem_sample, rel_bias, g_mix_pre, g_mix_post, w_in, lam_q1, lam_k1, lam_q2, lam_k2, g_a_out, g_b_q, g_b_k, g_c_q, g_c_kv, w_c_q_up, w_c_kv_up, w_br_a, w_br_b, w_br_c, w_mix_out, g_x_pre, g_x_post, g_mem, w_x_q, w_x_kv, w_x_out, g_ffn_pre, g_ffn_post, w_ffn_gate, w_ffn_up, w_ffn_down):
    depth = w_in.shape[0]
    d = x_prompt.shape[-1]
    tm_row = 512
    tq_a, tk_a = 256, 1024
    tq, tk = 512, 1024

    groups = [(x.reshape(-1, d), m, x.shape[0], x.shape[1])
              for x, m in ((x_prompt, mem_prompt), (x_sample, mem_sample))]
    tables = _rotary_tables(max(g[3] for g in groups))
    bias = _bias_tiles(rel_bias, tq_a, tk_a)
    row = lambda g: g.reshape(1, -1).astype(F32)
    in_scale = jnp.concatenate([jnp.full((1, Z_AK), A_QK_DIM ** -0.5 * LOG2E, F32),
                                jnp.ones((1, Z_COLS - Z_AK), F32)], axis=-1)
    ones_kv = jnp.ones((1, w_x_kv.shape[-1]), F32)
    kv_group = HEADS // B_KV_HEADS

    w_in_p = w_in_prep(jnp.swapaxes(w_in, 1, 2), tr=2048, tn=512)
    w_xkv_all = w_x_kv.astype(BF16)
    w_fg, w_fu, w_fd = w_ffn_gate.astype(BF16), w_ffn_up.astype(BF16), w_ffn_down.astype(BF16)

    xs = [g[0] for g in groups]
    for l in range(depth):
        lam_init = 0.8 - 0.6 * math.exp(-0.3 * l)
        lam = (jnp.exp(jnp.sum(lam_q1[l].astype(F32) * lam_k1[l].astype(F32)))
               - jnp.exp(jnp.sum(lam_q2[l].astype(F32) * lam_k2[l].astype(F32))) + lam_init).reshape(1)

        wq3 = w_c_q_up[l].reshape(C_Q_RANK, HEADS, C_NOPE + C_ROPE)
        w_cq = jnp.concatenate(
            [wq3[..., :C_NOPE].reshape(C_Q_RANK, HEADS * C_NOPE),
             jnp.pad(wq3[..., C_NOPE:], ((0, 0), (0, 0), (0, HEAD_DIM - C_ROPE))).reshape(
                 C_Q_RANK, HEADS * HEAD_DIM)], axis=-1).astype(BF16)
        w_ckv = w_c_kv_up[l].astype(BF16)
        w_a, w_b, w_c = w_br_a[l].astype(BF16), w_br_b[l].astype(BF16), w_br_c[l].astype(BF16)
        w_mix = w_mix_out[l].astype(BF16)
        w_xq, w_xo = w_x_q[l].astype(BF16), w_x_out[l].astype(BF16)

        for gi, (_, mem, nseq, seq) in enumerate(groups):
            x = xs[gi]
            blocks_per_seq = seq // tm_row
            streams = 2
            z = rms_matmul(x, row(g_mix_pre[l]), w_in_p, l, in_scale, tm=1024, tn=512)
            qb, kb, cq, ckv, kr = mixer_prep(z, tables, lambda i: i % blocks_per_seq, row(g_b_q[l]), row(g_b_k[l]),
                                             row(g_c_q[l]), row(g_c_kv[l]), w_cq, w_ckv, tm_row)
            oa = attention_a(lam, z, bias, row(g_a_out[l]), nseq, seq, tq_a, tk_a, 1.0 - lam_init, streams)
            ob = attention([(qb, lambda h: h)],
                           [(kb, lambda h: h // kv_group)],
                           (z, lambda h: Z_BV // HEAD_DIM + h // kv_group),
                           nseq, seq, tq, tk, streams)
            oc = attention([(cq, lambda h: h), (cq, lambda h: HEADS + h)],
                           [(ckv, lambda h: 2 * h), (kr, lambda h: 0)],
                           (ckv, lambda h: 2 * h + 1),
                           nseq, seq, tq, tk, streams)
            merged = gated_merge(oa, ob, oc, w_a, w_b, w_c, z, tm=1024, tn=512)
            x = mm_postnorm_residual(merged, w_mix, row(g_mix_post[l]), x, tm=tm_row)

            mem2 = mem.reshape(nseq * MEM_TOKENS, d)
            memkv = rms_matmul(mem2, row(g_mem[l]), w_xkv_all, l, ones_kv, tm=mem2.shape[0], tn=512)
            memkv = memkv.reshape(nseq, MEM_TOKENS, -1)
            x = memory_xattn(x, row(g_x_pre[l]), w_xq, memkv, lambda i: i // blocks_per_seq, w_xo,
                             row(g_x_post[l]), tm_row)
            xs[gi] = swiglu_ffn(x, row(g_ffn_pre[l]), w_fg, w_fu, w_fd, l, row(g_ffn_post[l]), tm=tm_row, tf=512)

    return tuple(x.reshape(g[2], g[3], d) for x, g in zip(xs, groups))
```

```python
import functools
import math

import jax
import jax.numpy as jnp
import numpy as np
from jax import lax
from jax.experimental import pallas as pl
from jax.experimental.pallas import tpu as pltpu

F32 = jnp.float32
BF16 = jnp.bfloat16

EPS = 1e-6
ROPE_THETA = 10000.0
LOG2E = 1.4426950408889634
GRID_W = 64
NUM_BUCKETS = 32
MAX_DISTANCE = 128

D_MODEL = 2048
HEADS = 8
HEAD_DIM = 128
A_QK_DIM = 64
B_KV_HEADS = 2
C_Q_RANK = 512
C_KV_RANK = 256
C_NOPE = 128
C_ROPE = 64
X_HEADS = 4
MEM_TOKENS = 256

Z_AQ, Z_AK, Z_AV, Z_BQ, Z_BK, Z_BV, Z_CQA, Z_CKVA, Z_CKR = 0, 1024, 2048, 3072, 4096, 4352, 4608, 5120, 5376
Z_RAW_SPLIT = 5440
Z_GATES = 5632
Z_COLS = Z_GATES + 3 * D_MODEL

VMEM_LIMIT = 56 * 1024 * 1024


def _cparams(sem):
    return pltpu.CompilerParams(dimension_semantics=sem, vmem_limit_bytes=VMEM_LIMIT)


def _rms(x, g):
    return x * lax.rsqrt(jnp.mean(x * x, axis=-1, keepdims=True) + EPS) * g


def _swap32(x):
    lane = lax.broadcasted_iota(jnp.int32, x.shape, 1)
    return jnp.where((lane % 64) < 32, pltpu.roll(x, 96, 1), pltpu.roll(x, 32, 1))


def _rope(x, cos, sin_signed):
    return x * cos + _swap32(x) * sin_signed


def _rms_matmul_kernel(x_ref, g_ref, w_ref, cs_ref, o_ref, xn_ref):
    @pl.when(pl.program_id(1) == 0)
    def _():
        xn_ref[...] = _rms(x_ref[...].astype(F32), g_ref[...]).astype(BF16)

    acc = jnp.dot(xn_ref[...], w_ref[...], preferred_element_type=F32)
    o_ref[...] = (acc * cs_ref[...]).astype(o_ref.dtype)


def rms_matmul(x, g, w, layer, colscale, tm, tn, out_dtype=BF16):
    m, k = x.shape
    n = w.shape[2]
    return pl.pallas_call(
        _rms_matmul_kernel,
        grid=(m // tm, n // tn),
        in_specs=[
            pl.BlockSpec((tm, k), lambda i, j: (i, 0)),
            pl.BlockSpec((1, k), lambda i, j: (0, 0)),
            pl.BlockSpec((None, k, tn), lambda i, j: (layer, 0, j)),
            pl.BlockSpec((1, tn), lambda i, j: (0, j)),
        ],
        out_specs=pl.BlockSpec((tm, tn), lambda i, j: (i, j)),
        out_shape=jax.ShapeDtypeStruct((m, n), out_dtype),
        scratch_shapes=[pltpu.VMEM((tm, k), BF16)],
        compiler_params=_cparams(("parallel", "arbitrary")),
        name="rms_matmul",
    )(x, g, w, colscale)


def _w_in_prep_kernel(lo_ref, hi_ref, o_ref, *, split_block, split_row):
    j = pl.program_id(2)

    def emit(rows):
        o_ref[...] = rows.T.astype(BF16)

    @pl.when(j < split_block)
    def _():
        emit(hi_ref[...])

    @pl.when(j == split_block)
    def _():
        row = lax.broadcasted_iota(jnp.int32, hi_ref.shape, 0)
        emit(jnp.where(row < split_row, hi_ref[...], 0.0))

    @pl.when(j > split_block)
    def _():
        emit(jnp.concatenate([lo_ref[split_row:, :], hi_ref[:split_row, :]], axis=0))


def w_in_prep(w_in_t, tr, tn):
    depth, raw, d = w_in_t.shape
    pad = Z_GATES - Z_RAW_SPLIT
    assert Z_COLS == raw + pad and pad < tn
    split_block, split_row = divmod(Z_RAW_SPLIT, tn)
    assert split_row % 8 == 0
    last = (raw - 1) // tn
    return pl.pallas_call(
        functools.partial(_w_in_prep_kernel, split_block=split_block, split_row=split_row),
        grid=(depth, d // tr, Z_COLS // tn),
        in_specs=[
            pl.BlockSpec((None, tn, tr), lambda l, i, j: (l, jnp.where(j > split_block, j - 1, 0), i)),
            pl.BlockSpec((None, tn, tr), lambda l, i, j: (l, jnp.minimum(j, last), i)),
        ],
        out_specs=pl.BlockSpec((None, tr, tn), lambda l, i, j: (l, i, j)),
        out_shape=jax.ShapeDtypeStruct((depth, d, Z_COLS), BF16),
        compiler_params=_cparams(("parallel", "parallel", "arbitrary")),
        name="w_in_prep",
    )(w_in_t, w_in_t)


def _prep_kernel(bq_ref, bkv_ref, cqa_ref, ckk_ref, cos_a_ref, sin_a_ref, cos_r_ref, sin_r_ref,
                 gbq_ref, gbk_ref, gcq_ref, gckv_ref, wq_ref, wkv_ref,
                 qb_out, kb_out, cq_out, ckv_out, kr_out, *, b_scale, c_scale):
    cos_a, sin_a = cos_a_ref[...], sin_a_ref[...]
    cos_r, sin_r = cos_r_ref[...], sin_r_ref[...]

    for h in range(HEADS):
        sl = slice(h * HEAD_DIM, (h + 1) * HEAD_DIM)
        y = _rms(bq_ref[:, sl].astype(F32), gbq_ref[...])
        qb_out[:, sl] = (_rope(y, cos_a, sin_a) * b_scale).astype(BF16)
    for n in range(B_KV_HEADS):
        sl = slice(n * HEAD_DIM, (n + 1) * HEAD_DIM)
        y = _rms(bkv_ref[:, sl].astype(F32), gbk_ref[...])
        kb_out[:, sl] = _rope(y, cos_a, sin_a).astype(BF16)

    xq = _rms(cqa_ref[...].astype(F32), gcq_ref[...]).astype(BF16)
    cq = jnp.dot(xq, wq_ref[...], preferred_element_type=F32)
    nope_cols = HEADS * C_NOPE
    cq_out[:, :nope_cols] = (cq[:, :nope_cols] * c_scale).astype(BF16)
    for h in range(HEADS):
        sl = slice(nope_cols + h * HEAD_DIM, nope_cols + (h + 1) * HEAD_DIM)
        cq_out[:, sl] = (_rope(cq[:, sl], cos_r, sin_r) * c_scale).astype(BF16)

    xkv = _rms(ckk_ref[:, :C_KV_RANK].astype(F32), gckv_ref[...]).astype(BF16)
    ckv_out[...] = jnp.dot(xkv, wkv_ref[...], preferred_element_type=F32).astype(BF16)
    kr = ckk_ref[:, C_KV_RANK:C_KV_RANK + HEAD_DIM].astype(F32)
    kr_out[...] = _rope(kr, cos_r, sin_r).astype(BF16)


def mixer_prep(z, tables, pos_block, gbq, gbk, gcq, gckv, wq, wkv, tm):
    t = z.shape[0]
    cos_a, sin_a, cos_r, sin_r = tables
    row = lambda w: pl.BlockSpec((tm, w), lambda i: (i, 0))
    tab = pl.BlockSpec((tm, HEAD_DIM), lambda i: (pos_block(i), 0))
    full = lambda a: pl.BlockSpec(a.shape, lambda i: (0,) * a.ndim)
    kern = functools.partial(_prep_kernel, b_scale=HEAD_DIM ** -0.5 * LOG2E,
                             c_scale=(C_NOPE + C_ROPE) ** -0.5 * LOG2E)
    return pl.pallas_call(
        kern,
        grid=(t // tm,),
        in_specs=[
            pl.BlockSpec((tm, 1024), lambda i: (i, Z_BQ // 1024)),
            pl.BlockSpec((tm, 512), lambda i: (i, Z_BK // 512)),
            pl.BlockSpec((tm, 512), lambda i: (i, Z_CQA // 512)),
            pl.BlockSpec((tm, 512), lambda i: (i, Z_CKVA // 512)),
            tab, tab, tab, tab,
            full(gbq), full(gbk), full(gcq), full(gckv), full(wq), full(wkv),
        ],
        out_specs=[row(1024), row(256), row(2048), row(2048), row(HEAD_DIM)],
        out_shape=[
            jax.ShapeDtypeStruct((t, 1024), BF16),
            jax.ShapeDtypeStruct((t, 256), BF16),
            jax.ShapeDtypeStruct((t, 2048), BF16),
            jax.ShapeDtypeStruct((t, 2048), BF16),
            jax.ShapeDtypeStruct((t, HEAD_DIM), BF16),
        ],
        compiler_params=_cparams(("parallel",)),
        name="mixer_prep",
    )(z, z, z, z, cos_a, sin_a, cos_r, sin_r, gbq, gbk, gcq, gckv, wq, wkv)


def _flash(q, k_refs, v_ref, s_refs, nk, tk, chunk_fn=None, bias_fn=None):
    m_rows = q.shape[0]
    dv = v_ref.shape[1]
    ones = jnp.ones((tk, dv), BF16)

    def rows_of(ref, chunk):
        if isinstance(chunk, int):
            return ref[chunk * tk:(chunk + 1) * tk, :]
        return ref[pl.ds(pl.multiple_of(chunk * tk, tk), tk), :]

    def scores(chunk):
        ks = [rows_of(r, chunk) for r in k_refs]
        k = ks[0] if len(ks) == 1 else jnp.concatenate(ks, axis=1)
        return lax.dot_general(q, k, (((1,), (1,)), ((), ())), preferred_element_type=F32)

    def update(pos, chunk, s, carry):
        m, acc = carry
        c = None
        if bias_fn is not None:
            s, c = bias_fn(pos, chunk, s)
        row_max = jnp.max(s, axis=-1, keepdims=True)
        if c is None:
            m_new = jnp.maximum(m, row_max)
            shift = m_new
        else:
            m_new = jnp.maximum(m, row_max + c)
            shift = m_new - c
        p = jnp.exp2((s - shift).astype(BF16))
        alpha = jnp.exp2(m - m_new)
        v_aug = jnp.concatenate([rows_of(v_ref, chunk), ones], axis=1)
        pv = jnp.dot(p, v_aug, preferred_element_type=F32)
        return m_new, alpha * acc + pv

    chunks = [pos if chunk_fn is None else chunk_fn(pos) for pos in range(nk)]
    carry = (jnp.full((m_rows, 1), -jnp.inf, F32), jnp.zeros((m_rows, 2 * dv), F32))
    s_refs[0][...] = scores(chunks[0])
    for pos in range(nk):
        if pos + 1 < nk:
            s_refs[(pos + 1) % 2][...] = scores(chunks[pos + 1])
        carry = update(pos, chunks[pos], s_refs[pos % 2][...], carry)
    m, acc = carry
    return m, acc[:, dv:], acc[:, :dv]


def _attn_kernel(*refs, n_q, n_k, nk, tk, streams):
    n_s = 2 * streams
    q_refs, k_refs = refs[:n_q], refs[n_q:n_q + n_k]
    v_ref, o_ref, s_refs = refs[n_q + n_k], refs[-n_s - 1], refs[-n_s:]
    rows = o_ref.shape[0] // streams
    for t in range(streams):
        sl = slice(t * rows, (t + 1) * rows)
        qs = [r[sl, :] for r in q_refs]
        q = qs[0] if n_q == 1 else jnp.concatenate(qs, axis=1)
        _, l, acc = _flash(q, k_refs, v_ref, s_refs[2 * t:2 * t + 2], nk, tk)
        o_ref[sl, :] = (acc / l).astype(o_ref.dtype)


def _attn_a_kernel(lam_ref, far_ref, q_ref, k_ref, v_ref, bias_ref, g_ref, *rest, tq, tk, nk, r, post_scale,
                   streams):
    n_s = 2 * streams
    o_ref, s_refs = rest[-n_s - 1], rest[-n_s:]
    h = pl.program_id(1)

    for t in range(streams):
        qi = pl.program_id(2) * streams + t
        diag = qi // r
        q = q_ref[t * tq:(t + 1) * tq, :]
        lane = lax.broadcasted_iota(jnp.int32, q.shape, 1)
        zero = jnp.zeros_like(q)
        q2 = jnp.concatenate([jnp.where(lane < A_QK_DIM, q, zero), jnp.where(lane >= A_QK_DIM, q, zero)], axis=0)

        def chunk_fn(pos, diag=diag):
            return (diag + pos) % nk

        def bias_fn(pos, chunk, s, qi=qi, diag=diag):
            tile = jnp.clip(chunk * r - qi, -r - 1, 2) + r + 1
            if pos == 0:
                b = bias_ref[0, tile]
                return jnp.concatenate([s[:tq] + b, s[tq:] + b], axis=0), None
            c = jnp.where(chunk > diag, far_ref[h, 1], far_ref[h, 0])
            edges = ([slice(0, 128)] if pos == 1 else []) + ([slice(tk - 128, tk)] if pos == nk - 1 else [])
            for cols in edges:
                d = bias_ref[0, tile, :, cols] - c
                parts = [s[:, :cols.start], s[:, cols] + jnp.concatenate([d, d], axis=0), s[:, cols.stop:]]
                s = jnp.concatenate([x for x in parts if x.shape[1]], axis=1)
            return s, c

        _, l, acc = _flash(q2, [k_ref], v_ref, s_refs[2 * t:2 * t + 2], nk, tk, chunk_fn, bias_fn)
        o = acc / l
        d = o[:tq] - lam_ref[0] * o[tq:]
        o_ref[t * tq:(t + 1) * tq, :] = (_rms(d, g_ref[...]) * post_scale).astype(o_ref.dtype)


def attention(q_parts, k_parts, v_part, nseq, seq, tq, tk, streams):
    rows = tq * streams
    nq = seq // rows
    in_specs, args = [], []
    for a, cf in q_parts:
        in_specs.append(pl.BlockSpec((rows, HEAD_DIM), lambda s, h, i, cf=cf: (s * nq + i, cf(h))))
        args.append(a)
    for a, cf in list(k_parts) + [v_part]:
        in_specs.append(pl.BlockSpec((seq, HEAD_DIM), lambda s, h, i, cf=cf: (s, cf(h))))
        args.append(a)
    kern = functools.partial(_attn_kernel, n_q=len(q_parts), n_k=len(k_parts), nk=seq // tk, tk=tk,
                             streams=streams)
    return pl.pallas_call(
        kern,
        grid=(nseq, HEADS, nq),
        in_specs=in_specs,
        out_specs=pl.BlockSpec((rows, HEAD_DIM), lambda s, h, i: (s * nq + i, h)),
        out_shape=jax.ShapeDtypeStruct((nseq * seq, HEADS * HEAD_DIM), BF16),
        scratch_shapes=[pltpu.VMEM((tq, tk), F32)] * (2 * streams),
        compiler_params=_cparams(("parallel", "parallel", "arbitrary")),
        name="attention",
    )(*args)


def attention_a(lam, z, bias, g, nseq, seq, tq, tk, post_scale, streams):
    rows = tq * streams
    nq = seq // rows
    r = tk // tq
    far = bias[:, :: r + 3, 0, 0]
    in_specs = [
        pl.BlockSpec(memory_space=pltpu.SMEM),
        pl.BlockSpec(memory_space=pltpu.SMEM),
        pl.BlockSpec((rows, HEAD_DIM), lambda s, h, i: (s * nq + i, Z_AQ // HEAD_DIM + h)),
        pl.BlockSpec((seq, HEAD_DIM), lambda s, h, i: (s, Z_AK // HEAD_DIM + h)),
        pl.BlockSpec((seq, HEAD_DIM), lambda s, h, i: (s, Z_AV // HEAD_DIM + h)),
        pl.BlockSpec((1, r + 4, tq, tk), lambda s, h, i: (h, 0, 0, 0)),
        pl.BlockSpec((1, HEAD_DIM), lambda s, h, i: (0, 0)),
    ]
    kern = functools.partial(_attn_a_kernel, tq=tq, tk=tk, nk=seq // tk, r=r, post_scale=post_scale,
                             streams=streams)
    return pl.pallas_call(
        kern,
        grid=(nseq, HEADS, nq),
        in_specs=in_specs,
        out_specs=pl.BlockSpec((rows, HEAD_DIM), lambda s, h, i: (s * nq + i, h)),
        out_shape=jax.ShapeDtypeStruct((nseq * seq, HEADS * HEAD_DIM), BF16),
        scratch_shapes=[pltpu.VMEM((2 * tq, tk), F32)] * (2 * streams),
        compiler_params=_cparams(("parallel", "parallel", "arbitrary")),
        name="attention_a",
    )(lam, far, z, z, z, bias, g)


def _merge_kernel(oa_ref, ob_ref, oc_ref, wa_ref, wb_ref, wc_ref, ga_ref, gb_ref, gc_ref, o_ref):
    def branch(o_r, w_r, g_r):
        y = jnp.dot(o_r[...], w_r[...], preferred_element_type=F32)
        return jax.nn.sigmoid(g_r[...].astype(F32)) * y

    merged = branch(oa_ref, wa_ref, ga_ref) + branch(ob_ref, wb_ref, gb_ref) + branch(oc_ref, wc_ref, gc_ref)
    o_ref[...] = merged.astype(o_ref.dtype)


def gated_merge(oa, ob, oc, wa, wb, wc, z, tm, tn):
    t, k = oa.shape
    n = wa.shape[1]
    o_spec = pl.BlockSpec((tm, k), lambda i, j: (i, 0))
    w_spec = pl.BlockSpec((k, tn), lambda i, j: (0, j))
    gate = lambda b: pl.BlockSpec((tm, tn), lambda i, j: (i, (Z_GATES + b * n) // tn + j))
    return pl.pallas_call(
        _merge_kernel,
        grid=(t // tm, n // tn),
        in_specs=[o_spec, o_spec, o_spec, w_spec, w_spec, w_spec, gate(0), gate(1), gate(2)],
        out_specs=pl.BlockSpec((tm, tn), lambda i, j: (i, j)),
        out_shape=jax.ShapeDtypeStruct((t, n), BF16),
        compiler_params=_cparams(("parallel", "arbitrary")),
        name="gated_merge",
    )(oa, ob, oc, wa, wb, wc, z, z, z)


def _mm_postnorm_kernel(a_ref, w_ref, g_ref, x_ref, o_ref):
    y = jnp.dot(a_ref[...], w_ref[...], preferred_element_type=F32)
    o_ref[...] = x_ref[...] + _rms(y, g_ref[...])


def mm_postnorm_residual(a, w, g, x, tm):
    t, k = a.shape
    n = w.shape[1]
    return pl.pallas_call(
        _mm_postnorm_kernel,
        grid=(t // tm,),
        in_specs=[
            pl.BlockSpec((tm, k), lambda i: (i, 0)),
            pl.BlockSpec((k, n), lambda i: (0, 0)),
            pl.BlockSpec((1, n), lambda i: (0, 0)),
            pl.BlockSpec((tm, n), lambda i: (i, 0)),
        ],
        out_specs=pl.BlockSpec((tm, n), lambda i: (i, 0)),
        out_shape=jax.ShapeDtypeStruct((t, n), F32),
        compiler_params=_cparams(("parallel",)),
        name="mm_postnorm_residual",
    )(a, w, g, x)


def _xattn_kernel(x_ref, gpre_ref, wq_ref, kv_ref, wo_ref, gpost_ref, o_ref, *, q_scale):
    x = x_ref[...]
    h = _rms(x, gpre_ref[...]).astype(BF16)
    q = (jnp.dot(h, wq_ref[...], preferred_element_type=F32) * q_scale).astype(BF16)
    kv_cols = X_HEADS * HEAD_DIM
    outs = []
    for hd in range(X_HEADS):
        sl = slice(hd * HEAD_DIM, (hd + 1) * HEAD_DIM)
        kh = kv_ref[0, :, sl]
        vh = kv_ref[0, :, kv_cols + hd * HEAD_DIM:kv_cols + (hd + 1) * HEAD_DIM]
        s = lax.dot_general(q[:, sl], kh, (((1,), (1,)), ((), ())), preferred_element_type=F32)
        p = jnp.exp2(s - jnp.max(s, axis=-1, keepdims=True))
        l = jnp.sum(p, axis=-1, keepdims=True)
        outs.append((jnp.dot(p.astype(BF16), vh, preferred_element_type=F32) / l).astype(BF16))
    o = jnp.concatenate(outs, axis=1)
    y = jnp.dot(o, wo_ref[...], preferred_element_type=F32)
    o_ref[...] = x + _rms(y, gpost_ref[...])


def memory_xattn(x, gpre, wq, memkv, mem_block, wo, gpost, tm):
    t, d = x.shape
    full = lambda a: pl.BlockSpec(a.shape, lambda i: (0,) * a.ndim)
    kern = functools.partial(_xattn_kernel, q_scale=HEAD_DIM ** -0.5 * LOG2E)
    return pl.pallas_call(
        kern,
        grid=(t // tm,),
        in_specs=[
            pl.BlockSpec((tm, d), lambda i: (i, 0)),
            full(gpre), full(wq),
            pl.BlockSpec((1,) + memkv.shape[1:], lambda i: (mem_block(i), 0, 0)),
            full(wo), full(gpost),
        ],
        out_specs=pl.BlockSpec((tm, d), lambda i: (i, 0)),
        out_shape=jax.ShapeDtypeStruct((t, d), F32),
        compiler_params=_cparams(("parallel",)),
        name="memory_xattn",
    )(x, gpre, wq, memkv, wo, gpost)


def _ffn_kernel(x_ref, gpre_ref, wg_ref, wu_ref, wd_ref, gpost_ref, o_ref, h_ref, acc_ref):
    f = pl.program_id(1)

    @pl.when(f == 0)
    def _():
        h_ref[...] = _rms(x_ref[...], gpre_ref[...]).astype(BF16)
        acc_ref[...] = jnp.zeros_like(acc_ref)

    h = h_ref[...]
    a = jnp.dot(h, wg_ref[...], preferred_element_type=F32)
    b = jnp.dot(h, wu_ref[...], preferred_element_type=F32)
    t = (a * jax.nn.sigmoid(a) * b).astype(BF16)
    acc_ref[...] += jnp.dot(t, wd_ref[...], preferred_element_type=F32)

    @pl.when(f == pl.num_programs(1) - 1)
    def _():
        o_ref[...] = x_ref[...] + _rms(acc_ref[...], gpost_ref[...])


def swiglu_ffn(x, gpre, wg, wu, wd, layer, gpost, tm, tf):
    t, d = x.shape
    ff = wg.shape[2]
    return pl.pallas_call(
        _ffn_kernel,
        grid=(t // tm, ff // tf),
        in_specs=[
            pl.BlockSpec((tm, d), lambda i, f: (i, 0)),
            pl.BlockSpec((1, d), lambda i, f: (0, 0)),
            pl.BlockSpec((None, d, tf), lambda i, f: (layer, 0, f)),
            pl.BlockSpec((None, d, tf), lambda i, f: (layer, 0, f)),
            pl.BlockSpec((None, tf, d), lambda i, f: (layer, f, 0)),
            pl.BlockSpec((1, d), lambda i, f: (0, 0)),
        ],
        out_specs=pl.BlockSpec((tm, d), lambda i, f: (i, 0)),
        out_shape=jax.ShapeDtypeStruct((t, d), F32),
        scratch_shapes=[pltpu.VMEM((tm, d), BF16), pltpu.VMEM((tm, d), F32)],
        compiler_params=_cparams(("parallel", "arbitrary")),
        name="swiglu_ffn",
    )(x, gpre, wg, wu, wd, gpost)


def _rotary_tables(seq):
    pos = jnp.arange(seq)

    def cs(p):
        inv = ROPE_THETA ** (-jnp.arange(0, 64, 2, dtype=F32) / 64)
        ang = p.astype(F32)[:, None] * inv[None, :]
        c, s = jnp.cos(ang), jnp.sin(ang)
        return jnp.concatenate([c, c], axis=-1), jnp.concatenate([-s, s], axis=-1)

    c_row, s_row = cs(pos // GRID_W)
    c_col, s_col = cs(pos % GRID_W)
    c_tok, s_tok = cs(pos)
    return (jnp.concatenate([c_row, c_col], axis=-1), jnp.concatenate([s_row, s_col], axis=-1),
            jnp.concatenate([c_tok, c_tok], axis=-1), jnp.concatenate([s_tok, s_tok], axis=-1))


def _t5_bucket(rel):
    nb = NUM_BUCKETS // 2
    max_exact = nb // 2
    ret = (rel > 0).astype(jnp.int32) * nb
    n = jnp.abs(rel)
    nf = jnp.maximum(n, 1).astype(F32)
    large = max_exact + (jnp.log(nf / max_exact) / math.log(MAX_DISTANCE / max_exact)
                         * (nb - max_exact)).astype(jnp.int32)
    large = jnp.minimum(large, nb - 1)
    return ret + jnp.where(n < max_exact, n, large)


T5_FAR = 91


def _bias_kernel(tab_ref, o_ref, *, tq, tk, r):
    h = pl.program_id(0)
    c_left = tab_ref[h, NUM_BUCKETS // 2 - 1] * LOG2E
    c_right = tab_ref[h, NUM_BUCKETS - 1] * LOG2E
    for t in range(r + 4):
        off = (t - (r + 1)) * tq
        lo = min(max((-off - T5_FAR + 1) // 128 * 128, 0), tk)
        hi = min(max(-((off - tq - T5_FAR + 1) // 128) * 128, 0), tk)
        if lo > 0:
            o_ref[0, t, :, :lo] = jnp.full((tq, lo), c_left, F32)
        if hi < tk:
            o_ref[0, t, :, hi:] = jnp.full((tq, tk - hi), c_right, F32)
        if hi > lo:
            rel = (off + lo + lax.broadcasted_iota(jnp.int32, (tq, hi - lo), 1)
                   - lax.broadcasted_iota(jnp.int32, (tq, hi - lo), 0))
            bucket = _t5_bucket(rel)
            v = jnp.full((tq, hi - lo), tab_ref[h, 0], F32)
            for b in range(1, NUM_BUCKETS):
                v = jnp.where(bucket == b, tab_ref[h, b], v)
            o_ref[0, t, :, lo:hi] = v * LOG2E


def _bias_tiles(rel_bias, tq, tk):
    assert tq >= 128 and tk % tq == 0
    n = np.arange(T5_FAR, 1 << 16, dtype=np.float32)
    far_bucket = np.minimum(8 + (np.log(n / 8) / math.log(MAX_DISTANCE / 8) * 8).astype(np.int32), 15)
    assert NUM_BUCKETS == 32 and far_bucket.min() == 15
    r = tk // tq
    return pl.pallas_call(
        functools.partial(_bias_kernel, tq=tq, tk=tk, r=r),
        grid=(HEADS,),
        in_specs=[pl.BlockSpec(memory_space=pltpu.SMEM)],
        out_specs=pl.BlockSpec((1, r + 4, tq, tk), lambda h: (h, 0, 0, 0)),
        out_shape=jax.ShapeDtypeStruct((HEADS, r + 4, tq, tk), F32),
        compiler_params=_cparams(("parallel",)),
        name="bias_tiles",
    )(rel_bias.T.astype(F32))


def kernel(x_prompt, x_sample, mem_prompt, mem_sample, rel_bias, g_mix_pre, g_mix_post, w_in, lam_q1, lam_k1, lam_q2, lam_k2, g_a_out, g_b_q, g_b_k, g_c_q, g_c_kv, w_c_q_up, w_c_kv_up, w_br_a, w_br_b, w_br_c, w_mix_out, g_x_pre, g_x_post, g_mem, w_x_q, w_x_kv, w_x_out, g_ffn_pre, g_ffn_post, w_ffn_gate, w_ffn_up, w_ffn_down):
    depth = w_in.shape[0]
    d = x_prompt.shape[-1]
    tm_row = 512
    tq_a, tk_a = 256, 1024
    tq, tk = 512, 1024

    groups = [(x.reshape(-1, d), m, x.shape[0], x.shape[1])
              for x, m in ((x_prompt, mem_prompt), (x_sample, mem_sample))]
    tables = _rotary_tables(max(g[3] for g in groups))
    bias = _bias_tiles(rel_bias, tq_a, tk_a)
    row = lambda g: g.reshape(1, -1).astype(F32)
    in_scale = jnp.concatenate([jnp.full((1, Z_AK), A_QK_DIM ** -0.5 * LOG2E, F32),
                                jnp.ones((1, Z_COLS - Z_AK), F32)], axis=-1)
    ones_kv = jnp.ones((1, w_x_kv.shape[-1]), F32)
    kv_group = HEADS // B_KV_HEADS

    w_in_p = w_in_prep(jnp.swapaxes(w_in, 1, 2), tr=2048, tn=512)
    w_xkv_all = w_x_kv.astype(BF16)
    w_fg, w_fu, w_fd = w_ffn_gate.astype(BF16), w_ffn_up.astype(BF16), w_ffn_down.astype(BF16)

    xs = [g[0] for g in groups]
    for l in range(depth):
        lam_init = 0.8 - 0.6 * math.exp(-0.3 * l)
        lam = (jnp.exp(jnp.sum(lam_q1[l].astype(F32) * lam_k1[l].astype(F32)))
               - jnp.exp(jnp.sum(lam_q2[l].astype(F32) * lam_k2[l].astype(F32))) + lam_init).reshape(1)

        wq3 = w_c_q_up[l].reshape(C_Q_RANK, HEADS, C_NOPE + C_ROPE)
        w_cq = jnp.concatenate(
            [wq3[..., :C_NOPE].reshape(C_Q_RANK, HEADS * C_NOPE),
             jnp.pad(wq3[..., C_NOPE:], ((0, 0), (0, 0), (0, HEAD_DIM - C_ROPE))).reshape(
                 C_Q_RANK, HEADS * HEAD_DIM)], axis=-1).astype(BF16)
        w_ckv = w_c_kv_up[l].astype(BF16)
        w_a, w_b, w_c = w_br_a[l].astype(BF16), w_br_b[l].astype(BF16), w_br_c[l].astype(BF16)
        w_mix = w_mix_out[l].astype(BF16)
        w_xq, w_xo = w_x_q[l].astype(BF16), w_x_out[l].astype(BF16)

        for gi, (_, mem, nseq, seq) in enumerate(groups):
            x = xs[gi]
            blocks_per_seq = seq // tm_row
            streams = 4
            z = rms_matmul(x, row(g_mix_pre[l]), w_in_p, l, in_scale, tm=1024, tn=512)
            qb, kb, cq, ckv, kr = mixer_prep(z, tables, lambda i: i % blocks_per_seq, row(g_b_q[l]), row(g_b_k[l]),
                                             row(g_c_q[l]), row(g_c_kv[l]), w_cq, w_ckv, tm_row)
            oa = attention_a(lam, z, bias, row(g_a_out[l]), nseq, seq, tq_a, tk_a, 1.0 - lam_init, streams)
            ob = attention([(qb, lambda h: h)],
                           [(kb, lambda h: h // kv_group)],
                           (z, lambda h: Z_BV // HEAD_DIM + h // kv_group),
                           nseq, seq, tq, tk, streams)
            oc = attention([(cq, lambda h: h), (cq, lambda h: HEADS + h)],
                           [(ckv, lambda h: 2 * h), (kr, lambda h: 0)],
                           (ckv, lambda h: 2 * h + 1),
                           nseq, seq, tq, tk, streams)
            merged = gated_merge(oa, ob, oc, w_a, w_b, w_c, z, tm=1024, tn=512)
            x = mm_postnorm_residual(merged, w_mix, row(g_mix_post[l]), x, tm=tm_row)

            mem2 = mem.reshape(nseq * MEM_TOKENS, d)
            memkv = rms_matmul(mem2, row(g_mem[l]), w_xkv_all, l, ones_kv, tm=mem2.shape[0], tn=512)
            memkv = memkv.reshape(nseq, MEM_TOKENS, -1)
            x = memory_xattn(x, row(g_x_pre[l]), w_xq, memkv, lambda i: i // blocks_per_seq, w_xo,
                             row(g_x_post[l]), tm_row)
            xs[gi] = swiglu_ffn(x, row(g_ffn_pre[l]), w_fg, w_fu, w_fd, l, row(g_ffn_post[l]), tm=tm_row, tf=512)

    return tuple(x.reshape(g[2], g[3], d) for x, g in zip(xs, groups))
```
